```python
import jax, jax.numpy as jnp
from jax import lax
import numpy as np

D_MODEL = 1024
BATCH = 4
SEQ = 8192
DEPTH = 2
DEC_BATCH = 2
DEC_SEQ = 16384
PAST_LEN = 128

GRID_W = 64
SSM_HEADS = 16
SSM_HEAD_DIM = 64
SSM_WIDTH = SSM_HEADS * SSM_HEAD_DIM
SSM_GROUPS = 2
SSM_STATE = 128
SSM_CONV = 5
SSM_CHUNK = 128
SSM_CONV_CH = SSM_WIDTH + 2 * SSM_GROUPS * SSM_STATE
GLA_HEADS = 4
GLA_DK_HEAD = 128
GLA_DV_HEAD = 256
GLA_DK = GLA_HEADS * GLA_DK_HEAD
GLA_DV = GLA_HEADS * GLA_DV_HEAD
GLA_RANK = 16
GLA_NORMALIZER = 16.0
GLA_CHUNK = 64
NA_HEADS = 16
NA_HEAD_DIM = 64
NA_WIDTH = NA_HEADS * NA_HEAD_DIM
NA_KH = 8
NA_KW = 16
D_FF = 2816
N_BRANCH = 3
EPS = 1e-6

IN_SPLITS = (SSM_WIDTH,
             SSM_CONV_CH,
             SSM_HEADS,
             SSM_HEADS,
             GLA_DK,
             GLA_DK,
             GLA_DV,
             GLA_DV,
             GLA_RANK,
             GLA_RANK,
             NA_WIDTH,
             NA_WIDTH,
             NA_WIDTH,
             N_BRANCH * D_MODEL)
IN_WIDTH = sum(IN_SPLITS)

kernel_name = "hybrid_bidir_ssd_gla_natten_encoder"


def split_cols(u):
    outs = []
    o = 0
    for w in IN_SPLITS:
        outs.append(u[..., o:o + w])
        o += w
    return outs


def rmsnorm(x, w):
    x32 = x.astype(jnp.float32)
    y = x32 * lax.rsqrt(jnp.mean(x32 * x32, axis=-1, keepdims=True) + EPS)
    return (y * w.astype(jnp.float32)).astype(x.dtype)


def swiglu(x, w_in, w_out):
    a, b = jnp.split(x @ w_in, 2, axis=-1)
    return (jax.nn.silu(a) * b) @ w_out


def flip(t):
    return jnp.flip(t, axis=1)


def dwconv_centred(u, w, b):
    y = lax.conv_general_dilated(u, w[:, None, :].astype(u.dtype), window_strides=(1,),
                                 padding=[(SSM_CONV // 2, SSM_CONV // 2)],
                                 dimension_numbers=('NWC', 'WIO', 'NWC'),
                                 feature_group_count=u.shape[-1])
    return y + b.astype(u.dtype)


def ssd_chunked(x, dt, a, Bm, Cm):
    bsz, l, h, p = x.shape
    g, n = Bm.shape[2], Bm.shape[3]
    hg = h // g
    Q = SSM_CHUNK
    c = l // Q
    xr = (x * dt[..., None]).reshape(bsz, c, Q, g, hg, p)
    cs = jnp.cumsum((dt * a).reshape(bsz, c, Q, g, hg), axis=2)
    Bc = Bm.reshape(bsz, c, Q, g, n)
    Cc = Cm.reshape(bsz, c, Q, g, n)
    causal = jnp.tril(jnp.ones((Q, Q), dtype=bool))
    seg = cs[:, :, :, None] - cs[:, :, None, :]
    Lm = jnp.exp(jnp.where(causal[:, :, None, None], seg, -jnp.inf))
    cb = jnp.einsum('bcign,bcjgn->bcijg', Cc, Bc)
    y_diag = jnp.einsum('bcijg,bcijgk,bcjgkp->bcigkp', cb, Lm, xr)
    decay_states = jnp.exp(cs[:, :, -1:] - cs)
    states = jnp.einsum('bcjgn,bcjgk,bcjgkp->bcgkpn', Bc, decay_states, xr)
    chunk_decay = jnp.exp(cs[:, :, -1])

    def step(S, inp):
        st, dec = inp
        return S * dec[..., None, None] + st, S

    S0 = jnp.zeros((bsz, g, hg, p, n), jnp.float32)
    _, S_in = lax.scan(step, S0, (jnp.moveaxis(states, 1, 0), jnp.moveaxis(chunk_decay, 1, 0)))
    S_in = jnp.moveaxis(S_in, 0, 1)
    y_off = jnp.einsum('bcign,bcgkpn,bcigk->bcigkp', Cc, S_in, jnp.exp(cs))
    return (y_diag + y_off).reshape(bsz, l, h, p)


def ssd_mixer(z, xbc, dt_f, dt_b, conv_w, conv_b, dt_bias, a_log, d_skip, norm_w):
    bsz, l, _ = z.shape
    gn = SSM_GROUPS * SSM_STATE
    xbc = jax.nn.silu(dwconv_centred(xbc, conv_w, conv_b)).astype(jnp.float32)
    xs = xbc[..., :SSM_WIDTH].reshape(bsz, l, SSM_HEADS, SSM_HEAD_DIM)
    Bm = xbc[..., SSM_WIDTH:SSM_WIDTH + gn].reshape(bsz, l, SSM_GROUPS, SSM_STATE)
    Cm = xbc[..., SSM_WIDTH + gn:].reshape(bsz, l, SSM_GROUPS, SSM_STATE)
    a = -jnp.exp(a_log.astype(jnp.float32))
    dbias = dt_bias.astype(jnp.float32)
    dtf = jax.nn.softplus(dt_f.astype(jnp.float32) + dbias[0])
    dtb = jax.nn.softplus(dt_b.astype(jnp.float32) + dbias[1])
    y = ssd_chunked(xs, dtf, a[0], Bm, Cm) + flip(ssd_chunked(flip(xs), flip(dtb), a[1], flip(Bm), flip(Cm)))
    y = y + d_skip.astype(jnp.float32)[:, None] * xs
    y = y.reshape(bsz, l, SSM_WIDTH) * jax.nn.silu(z.astype(jnp.float32))
    yg = y.reshape(bsz, l, SSM_GROUPS, SSM_WIDTH // SSM_GROUPS)
    yg = yg * lax.rsqrt(jnp.mean(yg * yg, axis=-1, keepdims=True) + EPS)
    return (yg.reshape(bsz, l, SSM_WIDTH) * norm_w.astype(jnp.float32)).astype(z.dtype)


def gla_chunked(q, k, v, gk):
    bsz, l, h, dk = q.shape
    dv = v.shape[-1]
    Q = GLA_CHUNK
    c = l // Q
    q = q.reshape(bsz, c, Q, h, dk)
    k = k.reshape(bsz, c, Q, h, dk)
    v = v.reshape(bsz, c, Q, h, dv)
    bcum = jnp.cumsum(gk.reshape(bsz, c, Q, h, dk), axis=2)
    q_e = q * jnp.exp(bcum)
    k_e = k * jnp.exp(-bcum)
    causal = jnp.tril(jnp.ones((Q, Q), dtype=bool))
    att = jnp.where(causal, jnp.einsum('bcihd,bcjhd->bchij', q_e, k_e), 0.0)
    o_intra = jnp.einsum('bchij,bcjhv->bcihv', att, v)
    k_dec = k * jnp.exp(bcum[:, :, -1:] - bcum)
    dec = jnp.exp(bcum[:, :, -1])

    def step(S, inp):
        qe, kd, vc, dc = inp
        o = jnp.einsum('bihd,bhdv->bihv', qe, S)
        S = S * dc[..., None] + jnp.einsum('bjhd,bjhv->bhdv', kd, vc)
        return S, o

    S0 = jnp.zeros((bsz, h, dk, dv), jnp.float32)
    xs = (jnp.moveaxis(q_e, 1, 0), jnp.moveaxis(k_dec, 1, 0), jnp.moveaxis(v, 1, 0), jnp.moveaxis(dec, 1, 0))
    _, o_inter = lax.scan(step, S0, xs)
    o = o_intra + jnp.moveaxis(o_inter, 0, 1)
    return o.reshape(bsz, l, h, dv)


def gla_mixer(q, k, v, g, down_f, down_b, gate_up, gate_b, norm_w):
    bsz, l, _ = q.shape
    f32 = jnp.float32
    qh = q.astype(f32).reshape(bsz, l, GLA_HEADS, GLA_DK_HEAD) * (GLA_DK_HEAD ** -0.5)
    kh = k.astype(f32).reshape(bsz, l, GLA_HEADS, GLA_DK_HEAD)
    vh = v.astype(f32).reshape(bsz, l, GLA_HEADS, GLA_DV_HEAD)

    def log_gate(down, up, bias):
        logits = down.astype(f32) @ up.astype(f32) + bias.astype(f32)
        return (jax.nn.log_sigmoid(logits) / GLA_NORMALIZER).reshape(bsz, l, GLA_HEADS, GLA_DK_HEAD)

    gk_f = log_gate(down_f, gate_up[0], gate_b[0])
    gk_b = log_gate(down_b, gate_up[1], gate_b[1])
    o = gla_chunked(qh, kh, vh, gk_f) + flip(gla_chunked(flip(qh), flip(kh), flip(vh), flip(gk_b)))
    o = o * lax.rsqrt(jnp.mean(o * o, axis=-1, keepdims=True) + EPS) * norm_w.astype(f32)
    o = o.reshape(bsz, l, GLA_DV) * jax.nn.silu(g.astype(f32))
    return o.astype(q.dtype)


def na_mixer(q, k, v, rpb):
    bsz, l, _ = q.shape
    rows = l // GRID_W
    kh = min(NA_KH, rows)
    qg = q.reshape(bsz, rows, GRID_W, NA_HEADS, NA_HEAD_DIM) * (NA_HEAD_DIM ** -0.5)
    kg = k.reshape(bsz, rows, GRID_W, NA_HEADS, NA_HEAD_DIM)
    vg = v.reshape(bsz, rows, GRID_W, NA_HEADS, NA_HEAD_DIM)
    row_start = jnp.clip(jnp.arange(rows) - kh // 2, 0, rows - kh)
    col_start = jnp.clip(jnp.arange(GRID_W) - NA_KW // 2, 0, GRID_W - NA_KW)
    col_idx = col_start[:, None] + jnp.arange(NA_KW)
    col_off = col_idx - jnp.arange(GRID_W)[:, None] + (NA_KW - 1)

    def one_row(r):
        rs = row_start[r]
        k_rows = lax.dynamic_slice_in_dim(kg, rs, kh, axis=1)
        v_rows = lax.dynamic_slice_in_dim(vg, rs, kh, axis=1)
        k_win = k_rows[:, :, col_idx]
        v_win = v_rows[:, :, col_idx]
        q_row = lax.dynamic_index_in_dim(qg, r, axis=1, keepdims=False)
        s = jnp.einsum('bqhd,bkqwhd->bhqkw', q_row, k_win, preferred_element_type=jnp.float32)
        row_off = rs + jnp.arange(kh) - r + (NA_KH - 1)
        bias = rpb[:, row_off[None, :, None], col_off[:, None, :]]
        s = s + bias.astype(jnp.float32)[None]
        p = jax.nn.softmax(s.reshape(bsz, NA_HEADS, GRID_W, kh * NA_KW), axis=-1).reshape(s.shape)
        return jnp.einsum('bhqkw,bkqwhd->bqhd', p.astype(v_win.dtype), v_win)

    out = lax.map(one_row, jnp.arange(rows))
    return jnp.moveaxis(out, 0, 1).reshape(bsz, l, NA_WIDTH)


def setup_inputs(seed: int = 0) -> dict:
    key = jax.random.key(seed)
    ks = jax.random.split(key, 32)
    f32 = jnp.float32

    def nrm(k, shape, fan_in):
        return jax.random.normal(k, shape, f32) * (fan_in ** -0.5)

    def gain(k, shape):
        return 1.0 + 0.01 * jax.random.normal(k, shape, f32)

    u_dt = jax.random.uniform(ks[9], (DEPTH, 2, SSM_HEADS), f32)
    dt0 = jnp.exp(u_dt * (jnp.log(0.1) - jnp.log(0.001)) + jnp.log(0.001))
    dt_bias = dt0 + jnp.log(-jnp.expm1(-dt0))
    a_log = jnp.log(jax.random.uniform(ks[10], (DEPTH, 2, SSM_HEADS), f32, 1.0, 16.0))
    return {
        "x_prompt": jax.random.normal(ks[0], (BATCH, SEQ, D_MODEL), f32),
        "x_sample": jax.random.normal(ks[1], (DEC_BATCH, DEC_SEQ, D_MODEL), f32),
        "ln_ffn1_w": gain(ks[2], (DEPTH, D_MODEL)),
        "ffn1_w_in": nrm(ks[3], (DEPTH, D_MODEL, 2 * D_FF), D_MODEL),
        "ffn1_w_out": nrm(ks[4], (DEPTH, D_FF, D_MODEL), D_FF),
        "ln_mix_w": gain(ks[5], (DEPTH, D_MODEL)),
        "w_in": nrm(ks[6], (DEPTH, D_MODEL, IN_WIDTH), D_MODEL),
        "ssm_conv_w": nrm(ks[7], (DEPTH, SSM_CONV, SSM_CONV_CH), SSM_CONV),
        "ssm_conv_b": 0.02 * jax.random.normal(ks[8], (DEPTH, SSM_CONV_CH), f32),
        "ssm_dt_bias": dt_bias,
        "ssm_a_log": a_log,
        "ssm_d": gain(ks[11], (DEPTH, SSM_HEADS)),
        "ssm_norm_w": gain(ks[12], (DEPTH, SSM_WIDTH)),
        "gla_gate_up": nrm(ks[13], (DEPTH, 2, GLA_RANK, GLA_DK), GLA_RANK),
        "gla_gate_b": 0.02 * jax.random.normal(ks[14], (DEPTH, 2, GLA_DK), f32),
        "gla_norm_w": gain(ks[15], (DEPTH, GLA_DV_HEAD)),
        "na_rpb": 0.02 * jax.random.normal(ks[16], (DEPTH, NA_HEADS, 2 * NA_KH - 1, 2 * NA_KW - 1), f32),
        "w_branch_a": nrm(ks[17], (DEPTH, SSM_WIDTH, D_MODEL), SSM_WIDTH),
        "w_branch_b": nrm(ks[18], (DEPTH, GLA_DV, D_MODEL), GLA_DV),
        "w_branch_c": nrm(ks[19], (DEPTH, NA_WIDTH, D_MODEL), NA_WIDTH),
        "w_out": nrm(ks[20], (DEPTH, D_MODEL, D_MODEL), D_MODEL),
        "ln_ffn2_w": gain(ks[21], (DEPTH, D_MODEL)),
        "ffn2_w_in": nrm(ks[22], (DEPTH, D_MODEL, 2 * D_FF), D_MODEL),
        "ffn2_w_out": nrm(ks[23], (DEPTH, D_FF, D_MODEL), D_FF),
        "ln_final_w": gain(ks[24], (D_MODEL,)),
    }


def reference(x_prompt, x_sample, ln_ffn1_w, ffn1_w_in, ffn1_w_out, ln_mix_w, w_in,
              ssm_conv_w, ssm_conv_b, ssm_dt_bias, ssm_a_log, ssm_d, ssm_norm_w,
              gla_gate_up, gla_gate_b, gla_norm_w, na_rpb,
              w_branch_a, w_branch_b, w_branch_c, w_out,
              ln_ffn2_w, ffn2_w_in, ffn2_w_out, ln_final_w):

    def layer(x, i):
        x = x + 0.5 * swiglu(rmsnorm(x, ln_ffn1_w[i]), ffn1_w_in[i], ffn1_w_out[i])
        h = rmsnorm(x, ln_mix_w[i])
        (z, xbc, dt_f, dt_b, q_b, k_b, v_b, g_b, dn_f, dn_b,
         q_c, k_c, v_c, gates) = split_cols(h @ w_in[i])
        y_a = ssd_mixer(z, xbc, dt_f, dt_b, ssm_conv_w[i], ssm_conv_b[i], ssm_dt_bias[i],
                        ssm_a_log[i], ssm_d[i], ssm_norm_w[i])
        y_b = gla_mixer(q_b, k_b, v_b, g_b, dn_f, dn_b, gla_gate_up[i], gla_gate_b[i], gla_norm_w[i])
        y_c = na_mixer(q_c, k_c, v_c, na_rpb[i])
        g = jax.nn.sigmoid(gates.astype(jnp.float32)).astype(x.dtype)
        g = g.reshape(g.shape[:-1] + (N_BRANCH, D_MODEL))
        merged = (g[..., 0, :] * (y_a @ w_branch_a[i])
                  + g[..., 1, :] * (y_b @ w_branch_b[i])
                  + g[..., 2, :] * (y_c @ w_branch_c[i]))
        x = x + merged @ w_out[i]
        x = x + 0.5 * swiglu(rmsnorm(x, ln_ffn2_w[i]), ffn2_w_in[i], ffn2_w_out[i])
        return x

    def trunk(x):
        for i in range(DEPTH):
            x = layer(x, i)
        return rmsnorm(x, ln_final_w)

    y_prompt = trunk(x_prompt)
    y_sample = trunk(x_sample)
    return (y_prompt, y_sample)
```

```python
import functools

import jax
import jax.numpy as jnp
from jax import lax
from jax.experimental import pallas as pl
from jax.experimental.pallas import tpu as pltpu

F32 = jnp.float32
BF16 = jnp.bfloat16

D_MODEL = 1024
DEPTH = 2
GRID_W = 64
SSM_HEADS = 16
SSM_HEAD_DIM = 64
SSM_WIDTH = 1024
SSM_GROUPS = 2
SSM_STATE = 128
SSM_CONV = 5
SSM_CHUNK = 128
GLA_HEADS = 4
GLA_DK_HEAD = 128
GLA_DV_HEAD = 256
GLA_DK = 512
GLA_DV = 1024
GLA_RANK = 16
GLA_NORMALIZER = 16.0
GLA_CHUNK = 64
NA_HEADS = 16
NA_HEAD_DIM = 64
NA_KH = 8
NA_KW = 16
D_FF = 2816
EPS = 1e-6

LANES = 128
SUBLANES = 8
VMEM_BYTES = 64 * 1024 * 1024
MASK_NEG = -1e30

U_Z, U_XS, U_QK, U_V, U_G, U_QC, U_KC, U_VC, U_G0 = (i * 1024 for i in range(9))
U_BC = 11 * 1024
U_SMALL = U_BC + 512
U_WIDTH = U_SMALL + LANES
U_TN = U_WIDTH // 3

FFN_TM = 512
FFN_FC = 1408
PROJ_TM = 512
SSD_TQ = 256
GLA_TQ = 256
NA_ROWS = 8
NA_TQ = NA_ROWS * GRID_W
MERGE_TM = 512


def _cparams(sem, vmem_mb):
    return pltpu.CompilerParams(dimension_semantics=sem, vmem_limit_bytes=vmem_mb * 1024 * 1024)


def _resident(shape):
    nd = len(shape)
    return pl.BlockSpec(shape, lambda *_: (0,) * nd, pipeline_mode=pl.Buffered(1))


def _rms(x, w):
    return x * lax.rsqrt(jnp.mean(x * x, axis=-1, keepdims=True) + EPS) * w


def _silu(x):
    return x * jax.nn.sigmoid(x)


def _dot(a, b):
    return jnp.dot(a, b, preferred_element_type=F32)


def _dot_nt(a, b):
    return lax.dot_general(a, b, (((1,), (1,)), ((), ())), preferred_element_type=F32)


def _dot_tn(a, b):
    return lax.dot_general(a, b, (((0,), (0,)), ((), ())), preferred_element_type=F32)


def _cumsum_dot(tri_bf, x):
    hi = x.astype(BF16)
    r1 = x - hi.astype(F32)
    mid = r1.astype(BF16)
    lo = (r1 - mid.astype(F32)).astype(BF16)
    return _dot(tri_bf, hi) + _dot(tri_bf, mid) + _dot(tri_bf, lo)


def _ffn_body(x_ref, lnw_ref, win_ref, wout_ref, *rest, final_norm):
    if final_norm:
        lnf_ref, o_ref = rest
    else:
        (o_ref,) = rest
    x = x_ref[...]
    h = _rms(x, lnw_ref[...]).astype(BF16)
    acc = None
    for c in range(D_FF // FFN_FC):
        a = _dot(h, win_ref[:, c * FFN_FC:(c + 1) * FFN_FC])
        b = _dot(h, win_ref[:, D_FF + c * FFN_FC:D_FF + (c + 1) * FFN_FC])
        p = _dot((_silu(a) * b).astype(BF16), wout_ref[c * FFN_FC:(c + 1) * FFN_FC, :])
        acc = p if acc is None else acc + p
    y = x + 0.5 * acc
    if final_norm:
        y = _rms(y, lnf_ref[...])
    o_ref[...] = y


def _ffn(x, lnw, w_in, w_out, lnf=None):
    T = x.shape[0]
    tok = pl.BlockSpec((FFN_TM, D_MODEL), lambda i: (i, 0))
    in_specs = [tok, _resident((1, D_MODEL)), _resident((D_MODEL, 2 * D_FF)), _resident((D_FF, D_MODEL))]
    args = [x, lnw, w_in, w_out]
    if lnf is not None:
        in_specs.append(_resident((1, D_MODEL)))
        args.append(lnf)
    return pl.pallas_call(
        functools.partial(_ffn_body, final_norm=lnf is not None),
        grid=(T // FFN_TM,),
        in_specs=in_specs,
        out_specs=tok,
        out_shape=jax.ShapeDtypeStruct((T, D_MODEL), F32),
        compiler_params=_cparams(("parallel",), 56),
        name="ffn",
    )(*args)


def _inproj_body(x_ref, lnw_ref, w_ref, u_ref):
    h = _rms(x_ref[...], lnw_ref[...]).astype(BF16)
    u_ref[...] = _dot(h, w_ref[...])


def _inproj(x, lnw, w):
    T = x.shape[0]
    return pl.pallas_call(
        _inproj_body,
        grid=(U_WIDTH // U_TN, T // PROJ_TM),
        in_specs=[pl.BlockSpec((PROJ_TM, D_MODEL), lambda j, i: (i, 0)),
                  pl.BlockSpec((1, D_MODEL), lambda j, i: (0, 0)),
                  pl.BlockSpec((D_MODEL, U_TN), lambda j, i: (0, j))],
        out_specs=pl.BlockSpec((PROJ_TM, U_TN), lambda j, i: (i, j)),
        out_shape=jax.ShapeDtypeStruct((T, U_WIDTH), F32),
        compiler_params=_cparams(("arbitrary", "arbitrary"), 56),
        name="inproj",
    )(x, lnw, w)


def _ssd_chunk(x_get, Bc, Cc, sm, dtb, alog, S_ref, reverse):
    Q = SSM_CHUNK
    off = SSM_HEADS if reverse else 0
    last = 0 if reverse else Q - 1
    ii = lax.broadcasted_iota(jnp.int32, (Q, Q), 0)
    jj = lax.broadcasted_iota(jnp.int32, (Q, Q), 1)
    tri = (ii <= jj) if reverse else (ii >= jj)
    lo = jj < SSM_HEAD_DIM
    dt = jax.nn.softplus(sm + dtb)
    dA = dt * (-jnp.exp(alog))
    cs = _cumsum_dot(tri.astype(BF16), dA)
    csT = cs.T
    dtT = dt.T
    ecs = jnp.exp(cs)
    WT = jnp.exp(csT[:, last:last + 1] - csT) * dtT
    el = ecs[last:last + 1, :]
    ys = []
    for g in range(SSM_GROUPS):
        Bg = Bc[:, g * SSM_STATE:(g + 1) * SSM_STATE]
        Cg = Cc[:, g * SSM_STATE:(g + 1) * SSM_STATE]
        cb = _dot_nt(Cg.astype(BF16), Bg.astype(BF16))
        BgT = Bg.T
        for pp in range(SSM_HEADS // SSM_GROUPS // 2):
            hp = g * (SSM_HEADS // SSM_GROUPS // 2) + pp
            xb = x_get(hp).astype(BF16)
            Sp = S_ref[hp]
            rhs = jnp.concatenate([xb, Sp.astype(BF16)], axis=0)
            outs, dSs, scs = [], [], []
            for e in range(2):
                h = off + 2 * hp + e
                seg = cs[:, h:h + 1] - csT[h:h + 1, :]
                Lm = jnp.where(tri, jnp.exp(seg), 0.0)
                M = (cb * Lm * dtT[h:h + 1, :]).astype(BF16)
                Ce = (Cg * ecs[:, h:h + 1]).astype(BF16)
                outs.append(_dot(jnp.concatenate([M, Ce], axis=1), rhs))
                dSs.append(_dot((BgT * WT[h:h + 1, :]).astype(BF16), xb))
                scs.append(el[:, h:h + 1])
            ys.append(jnp.where(lo, outs[0], outs[1]))
            S_ref[hp] = Sp * jnp.where(lo, scs[0], scs[1]) + jnp.where(lo, dSs[0], dSs[1])
    return ys


def _conv_silu_rows(ext_ref, w_ref, b_ref, r0, c0, cw):
    acc = b_ref[:, c0:c0 + cw]
    for k in range(SSM_CONV):
        s = SUBLANES - SSM_CONV // 2 + k + r0
        acc = acc + ext_ref[s:s + SSM_CHUNK, c0:c0 + cw] * w_ref[k:k + 1, c0:c0 + cw]
    return _silu(acc)


def _ssd_fwd_body(xs_ref, xsp_ref, xsn_ref, bc_ref, bcp_ref, bcn_ref, sm_ref,
                  cwx_ref, cbx_ref, cwb_ref, cbb_ref, dtb_ref, alog_ref,
                  yf_ref, xa_ref, ba_ref, extx, extb, S_ref, *, nblk):
    c = pl.program_id(1)
    TQ = SSD_TQ

    @pl.when(c == 0)
    def _():
        S_ref[...] = jnp.zeros_like(S_ref)

    first = c == 0
    final = c == nblk - 1
    for ext, cur, prv, nxt in ((extx, xs_ref, xsp_ref, xsn_ref), (extb, bc_ref, bcp_ref, bcn_ref)):
        ext[0:SUBLANES, :] = jnp.where(first, 0.0, prv[...])
        ext[SUBLANES:SUBLANES + TQ, :] = cur[...]
        ext[SUBLANES + TQ:2 * SUBLANES + TQ, :] = jnp.where(final, 0.0, nxt[...])

    for q in range(TQ // SSM_CHUNK):
        r0 = q * SSM_CHUNK
        rows = slice(r0, r0 + SSM_CHUNK)
        for c0 in range(0, SSM_WIDTH, 256):
            xa_ref[rows, c0:c0 + 256] = _conv_silu_rows(extx, cwx_ref, cbx_ref, r0, c0, 256)
        for c0 in range(0, 512, 256):
            ba_ref[rows, c0:c0 + 256] = _conv_silu_rows(extb, cwb_ref, cbb_ref, r0, c0, 256)
        ys = _ssd_chunk(lambda hp: xa_ref[rows, hp * LANES:(hp + 1) * LANES],
                        ba_ref[rows, 0:256], ba_ref[rows, 256:512], sm_ref[rows, :],
                        dtb_ref[...], alog_ref[...], S_ref, reverse=False)
        for hp, y in enumerate(ys):
            yf_ref[rows, hp * LANES:(hp + 1) * LANES] = y


def _ssd_bwd_body(xa_ref, ba_ref, sm_ref, z_ref, yf_ref, dtb_ref, alog_ref, dsk_ref, nw_ref,
                  ya_ref, S_ref):
    c = pl.program_id(1)
    TQ = SSD_TQ

    @pl.when(c == 0)
    def _():
        S_ref[...] = jnp.zeros_like(S_ref)

    half = SSM_WIDTH // SSM_GROUPS
    for q in reversed(range(TQ // SSM_CHUNK)):
        rows = slice(q * SSM_CHUNK, (q + 1) * SSM_CHUNK)
        ys = _ssd_chunk(lambda hp: xa_ref[rows, hp * LANES:(hp + 1) * LANES],
                        ba_ref[rows, 0:256], ba_ref[rows, 256:512], sm_ref[rows, :],
                        dtb_ref[...], alog_ref[...], S_ref, reverse=True)
        for g in range(SSM_GROUPS):
            ts = []
            ssq = None
            for pp in range(4):
                hp = g * 4 + pp
                cols = slice(hp * LANES, (hp + 1) * LANES)
                t = ys[hp] + yf_ref[rows, cols] + dsk_ref[:, cols] * xa_ref[rows, cols]
                t = t * _silu(z_ref[rows, cols])
                ts.append(t)
                s = jnp.sum(t * t, axis=-1, keepdims=True)
                ssq = s if ssq is None else ssq + s
            inv = lax.rsqrt(ssq * (1.0 / half) + EPS)
            for pp in range(4):
                cols = slice((g * 4 + pp) * LANES, (g * 4 + pp + 1) * LANES)
                ya_ref[rows, cols] = ts[pp] * inv * nw_ref[:, cols]


def _ssd(u, B, L, p):
    T = B * L
    TQ = SSD_TQ
    nblk = L // TQ
    hb = TQ // SUBLANES
    nh = T // SUBLANES

    def rowf(b, c):
        return b * nblk + c

    def rowb(b, c):
        return b * nblk + (nblk - 1 - c)

    def col(width, colstart, rowfn):
        return pl.BlockSpec((TQ, width), lambda b, c: (rowfn(b, c), colstart // width))

    def halo(width, colstart, shift):
        def im(b, c):
            r = (b * nblk + c + shift) * hb - (1 - shift)
            return (jnp.clip(r, 0, nh - 1), colstart // width)
        return pl.BlockSpec((SUBLANES, width), im)

    state = pltpu.VMEM((SSM_HEADS // 2, SSM_STATE, LANES), F32)
    yf, xa, ba = pl.pallas_call(
        functools.partial(_ssd_fwd_body, nblk=nblk),
        grid=(B, nblk),
        in_specs=[col(1024, U_XS, rowf), halo(1024, U_XS, 0), halo(1024, U_XS, 1),
                  col(512, U_BC, rowf), halo(512, U_BC, 0), halo(512, U_BC, 1),
                  col(LANES, U_SMALL, rowf),
                  _resident((SUBLANES, 1024)), _resident((1, 1024)),
                  _resident((SUBLANES, 512)), _resident((1, 512)),
                  _resident((1, LANES)), _resident((1, LANES))],
        out_specs=[pl.BlockSpec((TQ, 1024), lambda b, c: (rowf(b, c), 0)),
                   pl.BlockSpec((TQ, 1024), lambda b, c: (rowf(b, c), 0)),
                   pl.BlockSpec((TQ, 512), lambda b, c: (rowf(b, c), 0))],
        out_shape=[jax.ShapeDtypeStruct((T, 1024), F32), jax.ShapeDtypeStruct((T, 1024), F32),
                   jax.ShapeDtypeStruct((T, 512), F32)],
        scratch_shapes=[pltpu.VMEM((TQ + 2 * SUBLANES, 1024), F32),
                        pltpu.VMEM((TQ + 2 * SUBLANES, 512), F32), state],
        compiler_params=_cparams(("arbitrary", "arbitrary"), 48),
        name="ssd_fwd",
    )(u, u, u, u, u, u, u, p["cwx"], p["cbx"], p["cwb"], p["cbb"], p["dtb"], p["alog"])

    def rb(width):
        return pl.BlockSpec((TQ, width), lambda b, c: (rowb(b, c), 0))

    return pl.pallas_call(
        _ssd_bwd_body,
        grid=(B, nblk),
        in_specs=[rb(1024), rb(512), col(LANES, U_SMALL, rowb), col(1024, U_Z, rowb), rb(1024),
                  _resident((1, LANES)), _resident((1, LANES)),
                  _resident((1, 1024)), _resident((1, 1024))],
        out_specs=rb(1024),
        out_shape=jax.ShapeDtypeStruct((T, 1024), F32),
        scratch_shapes=[state],
        compiler_params=_cparams(("arbitrary", "arbitrary"), 48),
        name="ssd_bwd",
    )(xa, ba, u, u, yf, p["dtb"], p["alog"], p["dskip"], p["ssm_nw"])


def _gla_chunk(q, k, v_get, gk, ST_ref, reverse):
    Q = GLA_CHUNK
    last = 0 if reverse else Q - 1
    ii = lax.broadcasted_iota(jnp.int32, (Q, Q), 0)
    jj = lax.broadcasted_iota(jnp.int32, (Q, Q), 1)
    tri = (ii <= jj) if reverse else (ii >= jj)
    bcum = _cumsum_dot(tri.astype(BF16), gk)
    bl = bcum[last:last + 1, :]
    qe = ((q * (GLA_DK_HEAD ** -0.5)) * jnp.exp(bcum)).astype(BF16)
    ke = (k * jnp.exp(-bcum)).astype(BF16)
    kd = (k * jnp.exp(bl - bcum)).astype(BF16)
    dec = jnp.exp(bl)
    outs = []
    for h in range(GLA_HEADS):
        sl = slice(h * GLA_DK_HEAD, (h + 1) * GLA_DK_HEAD)
        att = jnp.where(tri, _dot_nt(qe[:, sl], ke[:, sl]), 0.0).astype(BF16)
        vh = v_get(h).astype(BF16)
        ST = ST_ref[h]
        outs.append(_dot(att, vh) + _dot_nt(qe[:, sl], ST.astype(BF16)))
        ST_ref[h] = ST * dec[:, sl] + _dot_tn(vh, kd[:, sl])
    return outs


def _gla_log_gate(sm_rows, gup_ref, gb_ref):
    logits = _dot(sm_rows.astype(BF16), gup_ref[...]) + gb_ref[...]
    return jax.nn.log_sigmoid(logits) / GLA_NORMALIZER


def _gla_fwd_body(qk_ref, v_ref, sm_ref, gup_ref, gb_ref, of_ref, ST_ref):
    c = pl.program_id(1)

    @pl.when(c == 0)
    def _():
        ST_ref[...] = jnp.zeros_like(ST_ref)

    for q in range(GLA_TQ // GLA_CHUNK):
        rows = slice(q * GLA_CHUNK, (q + 1) * GLA_CHUNK)
        gk = _gla_log_gate(sm_ref[rows, :], gup_ref, gb_ref)
        outs = _gla_chunk(qk_ref[rows, 0:GLA_DK], qk_ref[rows, GLA_DK:2 * GLA_DK],
                          lambda h: v_ref[rows, h * GLA_DV_HEAD:(h + 1) * GLA_DV_HEAD],
                          gk, ST_ref, reverse=False)
        for h, o in enumerate(outs):
            of_ref[rows, h * GLA_DV_HEAD:(h + 1) * GLA_DV_HEAD] = o


def _gla_bwd_body(qk_ref, v_ref, sm_ref, g_ref, of_ref, gup_ref, gb_ref, nw_ref, yb_ref, ST_ref):
    c = pl.program_id(1)

    @pl.when(c == 0)
    def _():
        ST_ref[...] = jnp.zeros_like(ST_ref)

    for q in reversed(range(GLA_TQ // GLA_CHUNK)):
        rows = slice(q * GLA_CHUNK, (q + 1) * GLA_CHUNK)
        gk = _gla_log_gate(sm_ref[rows, :], gup_ref, gb_ref)
        outs = _gla_chunk(qk_ref[rows, 0:GLA_DK], qk_ref[rows, GLA_DK:2 * GLA_DK],
                          lambda h: v_ref[rows, h * GLA_DV_HEAD:(h + 1) * GLA_DV_HEAD],
                          gk, ST_ref, reverse=True)
        for h, ob in enumerate(outs):
            cols = slice(h * GLA_DV_HEAD, (h + 1) * GLA_DV_HEAD)
            o = ob + of_ref[rows, cols]
            o = o * lax.rsqrt(jnp.mean(o * o, axis=-1, keepdims=True) + EPS) * nw_ref[...]
            yb_ref[rows, cols] = o * _silu(g_ref[rows, cols])


def _gla(u, B, L, p):
    T = B * L
    TQ = GLA_TQ
    nblk = L // TQ

    def rowf(b, c):
        return b * nblk + c

    def rowb(b, c):
        return b * nblk + (nblk - 1 - c)

    def col(width, colstart, rowfn):
        return pl.BlockSpec((TQ, width), lambda b, c: (rowfn(b, c), colstart // width))

    state = pltpu.VMEM((GLA_HEADS, GLA_DV_HEAD, GLA_DK_HEAD), F32)
    of = pl.pallas_call(
        _gla_fwd_body,
        grid=(B, nblk),
        in_specs=[col(1024, U_QK, rowf), col(1024, U_V, rowf), col(LANES, U_SMALL, rowf),
                  _resident((LANES, GLA_DK)), _resident((1, GLA_DK))],
        out_specs=pl.BlockSpec((TQ, 1024), lambda b, c: (rowf(b, c), 0)),
        out_shape=jax.ShapeDtypeStruct((T, GLA_DV), F32),
        scratch_shapes=[state],
        compiler_params=_cparams(("arbitrary", "arbitrary"), 48),
        name="gla_fwd",
    )(u, u, u, p["gup_f"], p["gb_f"])
    return pl.pallas_call(
        _gla_bwd_body,
        grid=(B, nblk),
        in_specs=[col(1024, U_QK, rowb), col(1024, U_V, rowb), col(LANES, U_SMALL, rowb),
                  col(1024, U_G, rowb), pl.BlockSpec((TQ, 1024), lambda b, c: (rowb(b, c), 0)),
                  _resident((LANES, GLA_DK)), _resident((1, GLA_DK)), _resident((1, GLA_DV_HEAD))],
        out_specs=pl.BlockSpec((TQ, 1024), lambda b, c: (rowb(b, c), 0)),
        out_shape=jax.ShapeDtypeStruct((T, GLA_DV), F32),
        scratch_shapes=[state],
        compiler_params=_cparams(("arbitrary", "arbitrary"), 48),
        name="gla_bwd",
    )(u, u, u, u, of, p["gup_b"], p["gb_b"], p["gla_nw"])


def _na_body(q_ref, kp_ref, kc_ref, kn_ref, vp_ref, vc_ref, vn_ref, tab_ref, o_ref, kbuf, vbuf, *, nblk):
    i = pl.program_id(1)
    TQ = NA_TQ
    rows_total = nblk * NA_ROWS
    for buf, refs in ((kbuf, (kp_ref, kc_ref, kn_ref)), (vbuf, (vp_ref, vc_ref, vn_ref))):
        for n, r in enumerate(refs):
            buf[n * TQ:(n + 1) * TQ, :] = r[...].astype(BF16)

    lane = lax.broadcasted_iota(jnp.int32, (GRID_W, LANES), 1)
    lo = lane < NA_HEAD_DIM
    nkeys = NA_KH * GRID_W

    def row_body(rho, carry):
        r = i * NA_ROWS + rho
        rs = jnp.clip(r - NA_KH // 2, 0, rows_total - NA_KH)
        delta = r - rs
        base = pl.multiple_of((rs - (i - 1) * NA_ROWS) * GRID_W, GRID_W)
        qrows = pl.ds(pl.multiple_of(rho * GRID_W, GRID_W), GRID_W)
        for hp in range(NA_HEADS // 2):
            cols = slice(hp * LANES, (hp + 1) * LANES)
            qp = q_ref[qrows, cols] * (NA_HEAD_DIM ** -0.5)
            lhs = jnp.concatenate([jnp.where(lo, qp, 0.0), jnp.where(lo, 0.0, qp)], axis=0).astype(BF16)
            s = _dot_nt(lhs, kbuf[pl.ds(base, nkeys), cols])
            bias = jnp.concatenate(
                [tab_ref[2 * jp - delta + NA_KH - 1, hp] for jp in range(NA_KH // 2)], axis=1)
            s = s + bias
            m = jnp.max(s, axis=-1, keepdims=True)
            pexp = jnp.exp(s - m)
            l = jnp.sum(pexp, axis=-1, keepdims=True)
            o = _dot(pexp.astype(BF16), vbuf[pl.ds(base, nkeys), cols]) / l
            o_ref[qrows, cols] = jnp.where(lo, o[0:GRID_W], o[GRID_W:2 * GRID_W])
        return carry

    lax.fori_loop(0, NA_ROWS, row_body, 0)


def _na(u, B, L, tab):
    T = B * L
    TQ = NA_TQ
    nblk = L // TQ

    def blk(colstart, shift):
        def im(b, i):
            return (b * nblk + jnp.clip(i + shift, 0, nblk - 1), colstart // 1024)
        return pl.BlockSpec((TQ, 1024), im)

    return pl.pallas_call(
        functools.partial(_na_body, nblk=nblk),
        grid=(B, nblk),
        in_specs=[blk(U_QC, 0), blk(U_KC, -1), blk(U_KC, 0), blk(U_KC, 1),
                  blk(U_VC, -1), blk(U_VC, 0), blk(U_VC, 1),
                  _resident(tab.shape)],
        out_specs=pl.BlockSpec((TQ, 1024), lambda b, i: (b * nblk + i, 0)),
        out_shape=jax.ShapeDtypeStruct((T, 1024), F32),
        scratch_shapes=[pltpu.VMEM((3 * TQ, 1024), BF16), pltpu.VMEM((3 * TQ, 1024), BF16)],
        compiler_params=_cparams(("parallel", "parallel"), 56),
        name="na",
    )(u, u, u, u, u, u, u, tab)


def _na_bias_table(rpb):
    c = jnp.arange(GRID_W)[:, None]
    kc = jnp.arange(GRID_W)[None, :]
    cs = jnp.clip(c - NA_KW // 2, 0, GRID_W - NA_KW)
    valid = (kc >= cs) & (kc < cs + NA_KW)
    off = jnp.clip(kc - c + NA_KW - 1, 0, 2 * NA_KW - 2)
    t = jnp.where(valid, rpb.astype(F32)[:, :, off], MASK_NEG)
    tp = jnp.concatenate([t[:, :-1], t[:, 1:]], axis=-1)
    n = 2 * NA_KH - 2
    tp = tp.reshape(NA_HEADS // 2, 2, n, GRID_W, LANES).transpose(2, 0, 1, 3, 4)
    return tp.reshape(n, NA_HEADS // 2, 2 * GRID_W, LANES)


def _merge_body(x_ref, ya_ref, yb_ref, yc_ref, g0_ref, g1_ref, g2_ref, wa_ref, wb_ref, wc_ref, wo_ref, o_ref):
    m = jax.nn.sigmoid(g0_ref[...]) * _dot(ya_ref[...].astype(BF16), wa_ref[...])
    m = m + jax.nn.sigmoid(g1_ref[...]) * _dot(yb_ref[...].astype(BF16), wb_ref[...])
    m = m + jax.nn.sigmoid(g2_ref[...]) * _dot(yc_ref[...].astype(BF16), wc_ref[...])
    o_ref[...] = x_ref[...] + _dot(m.astype(BF16), wo_ref[...])


def _merge(x, ya, yb, yc, u, wa, wb, wc, wo):
    T = x.shape[0]
    tok = pl.BlockSpec((MERGE_TM, D_MODEL), lambda i: (i, 0))

    def gate(k):
        return pl.BlockSpec((MERGE_TM, D_MODEL), lambda i: (i, U_G0 // 1024 + k))

    w = _resident((D_MODEL, D_MODEL))
    return pl.pallas_call(
        _merge_body,
        grid=(T // MERGE_TM,),
        in_specs=[tok, tok, tok, tok, gate(0), gate(1), gate(2), w, w, w, w],
        out_specs=tok,
        out_shape=jax.ShapeDtypeStruct((T, D_MODEL), F32),
        compiler_params=_cparams(("parallel",), 48),
        name="merge",
    )(x, ya, yb, yc, u, u, u, wa, wb, wc, wo)


def _pad_rows(w, rows):
    return jnp.pad(w, ((0, rows - w.shape[0]), (0, 0)))


def _layer_params(i, ln_ffn1_w, ffn1_w_in, ffn1_w_out, ln_mix_w, w_in, ssm_conv_w, ssm_conv_b, ssm_dt_bias,
                  ssm_a_log, ssm_d, ssm_norm_w, gla_gate_up, gla_gate_b, gla_norm_w, na_rpb,
                  w_branch_a, w_branch_b, w_branch_c, w_out, ln_ffn2_w, ffn2_w_in, ffn2_w_out):
    w = w_in[i]
    o = 0
    parts = {}
    for name, width in (("z", 1024), ("xs", 1024), ("bc", 512), ("dt", 32), ("q", 512), ("k", 512),
                        ("v", 1024), ("g", 1024), ("dn", 32), ("qc", 1024), ("kc", 1024), ("vc", 1024),
                        ("gates", 3072)):
        parts[name] = w[:, o:o + width]
        o += width
    small = jnp.pad(jnp.concatenate([parts["dt"], parts["dn"]], axis=1), ((0, 0), (0, LANES - 64)))
    w_perm = jnp.concatenate([parts[n] for n in ("z", "xs", "q", "k", "v", "g", "qc", "kc", "vc", "gates", "bc")]
                             + [small], axis=1).astype(BF16)

    def gup(d):
        full = jnp.zeros((LANES, GLA_DK), F32).at[32 + 16 * d:48 + 16 * d].set(gla_gate_up[i, d].astype(F32))
        return full.astype(BF16)

    def pad_lanes(v2):
        return jnp.pad(v2.astype(F32).reshape(1, 2 * SSM_HEADS), ((0, 0), (0, LANES - 2 * SSM_HEADS)))

    return dict(
        ln1=ln_ffn1_w[i].reshape(1, -1), f1_in=ffn1_w_in[i].astype(BF16), f1_out=ffn1_w_out[i].astype(BF16),
        ln_mix=ln_mix_w[i].reshape(1, -1), w_perm=w_perm,
        cwx=_pad_rows(ssm_conv_w[i][:, :1024], SUBLANES), cbx=ssm_conv_b[i][:1024].reshape(1, -1),
        cwb=_pad_rows(ssm_conv_w[i][:, 1024:], SUBLANES), cbb=ssm_conv_b[i][1024:].reshape(1, -1),
        dtb=pad_lanes(ssm_dt_bias[i]), alog=pad_lanes(ssm_a_log[i]),
        dskip=jnp.repeat(ssm_d[i].astype(F32), SSM_HEAD_DIM).reshape(1, -1),
        ssm_nw=ssm_norm_w[i].reshape(1, -1),
        gup_f=gup(0), gup_b=gup(1), gb_f=gla_gate_b[i, 0].reshape(1, -1), gb_b=gla_gate_b[i, 1].reshape(1, -1),
        gla_nw=gla_norm_w[i].reshape(1, -1),
        na_tab=_na_bias_table(na_rpb[i]),
        wa=w_branch_a[i].astype(BF16), wb=w_branch_b[i].astype(BF16), wc=w_branch_c[i].astype(BF16),
        wo=w_out[i].astype(BF16),
        ln2=ln_ffn2_w[i].reshape(1, -1), f2_in=ffn2_w_in[i].astype(BF16), f2_out=ffn2_w_out[i].astype(BF16),
    )


def _trunk(x3, layers, ln_final):
    B, L, D = x3.shape
    x = x3.reshape(B * L, D)
    for li, p in enumerate(layers):
        x = _ffn(x, p["ln1"], p["f1_in"], p["f1_out"])
        u = _inproj(x, p["ln_mix"], p["w_perm"])
        ya = _ssd(u, B, L, p)
        yb = _gla(u, B, L, p)
        yc = _na(u, B, L, p["na_tab"])
        x = _merge(x, ya, yb, yc, u, p["wa"], p["wb"], p["wc"], p["wo"])
        x = _ffn(x, p["ln2"], p["f2_in"], p["f2_out"], ln_final if li == len(layers) - 1 else None)
    return x.reshape(B, L, D)


def kernel(x_prompt, x_sample, ln_ffn1_w, ffn1_w_in, ffn1_w_out, ln_mix_w, w_in, ssm_conv_w, ssm_conv_b, ssm_dt_bias, ssm_a_log, ssm_d, ssm_norm_w, gla_gate_up, gla_gate_b, gla_norm_w, na_rpb, w_branch_a, w_branch_b, w_branch_c, w_out, ln_ffn2_w, ffn2_w_in, ffn2_w_out, ln_final_w):
    layers = [_layer_params(i, ln_ffn1_w, ffn1_w_in, ffn1_w_out, ln_mix_w, w_in, ssm_conv_w, ssm_conv_b,
                            ssm_dt_bias, ssm_a_log, ssm_d, ssm_norm_w, gla_gate_up, gla_gate_b, gla_norm_w,
                            na_rpb, w_branch_a, w_branch_b, w_branch_c, w_out, ln_ffn2_w, ffn2_w_in, ffn2_w_out)
              for i in range(DEPTH)]
    ln_final = ln_final_w.reshape(1, -1)
    return (_trunk(x_prompt, layers, ln_final), _trunk(x_sample, layers, ln_final))
```

```python
import functools

import jax
import jax.numpy as jnp
from jax import lax
from jax.experimental import pallas as pl
from jax.experimental.pallas import tpu as pltpu

F32 = jnp.float32
BF16 = jnp.bfloat16

D_MODEL = 1024
DEPTH = 2
GRID_W = 64
SSM_HEADS = 16
SSM_HEAD_DIM = 64
SSM_WIDTH = 1024
SSM_GROUPS = 2
SSM_STATE = 128
SSM_CONV = 5
SSM_CHUNK = 128
GLA_HEADS = 4
GLA_DK_HEAD = 128
GLA_DV_HEAD = 256
GLA_DK = 512
GLA_DV = 1024
GLA_RANK = 16
GLA_NORMALIZER = 16.0
GLA_CHUNK = 64
NA_HEADS = 16
NA_HEAD_DIM = 64
NA_KH = 8
NA_KW = 16
D_FF = 2816
EPS = 1e-6

LANES = 128
SUBLANES = 8
BF16_ROWS = 16
MASK_NEG = -1e30

U_Z, U_XS, U_QK, U_V, U_G, U_QC, U_KC, U_VC, U_G0 = (i * 1024 for i in range(9))
U_BC = 11 * 1024
U_WIDTH = U_BC + 512
U_TN = U_WIDTH // 4
SMALL_W = LANES

FFN_TM = 512
FFN_FC = 1408
PROJ_TM = 1024
SSD_TQ = 256
GLA_TQ = 256
NA_ROWS = 8
NA_TQ = NA_ROWS * GRID_W
MERGE_TM = 512


def _cparams(sem, vmem_mb):
    return pltpu.CompilerParams(dimension_semantics=sem, vmem_limit_bytes=vmem_mb * 1024 * 1024)


def _resident(shape):
    nd = len(shape)
    return pl.BlockSpec(shape, lambda *_: (0,) * nd, pipeline_mode=pl.Buffered(1))


def _rms(x, w):
    return x * lax.rsqrt(jnp.mean(x * x, axis=-1, keepdims=True) + EPS) * w


def _silu(x):
    return x * jax.nn.sigmoid(x)


def _dot(a, b):
    return jnp.dot(a, b, preferred_element_type=F32)


def _dot_nt(a, b):
    return lax.dot_general(a, b, (((1,), (1,)), ((), ())), preferred_element_type=F32)


def _dot_tn(a, b):
    return lax.dot_general(a, b, (((0,), (0,)), ((), ())), preferred_element_type=F32)


def _split3(x):
    hi = x.astype(BF16)
    r1 = x - hi.astype(F32)
    mid = r1.astype(BF16)
    lo = (r1 - mid.astype(F32)).astype(BF16)
    return hi, mid, lo


def _cumsum_dot(tri_bf, x):
    hi, mid, lo = _split3(x)
    return _dot(tri_bf, hi) + _dot(tri_bf, mid) + _dot(tri_bf, lo)


def _ffn_body(x_ref, lnw_ref, win_ref, wout_ref, *rest, final_norm, emit_mix):
    rest = list(rest)
    lnf_ref = rest.pop(0) if final_norm else None
    if emit_mix:
        lnm_ref, wsm_ref = rest.pop(0), rest.pop(0)
    o_ref = rest.pop(0)
    x = x_ref[...]
    h = _rms(x, lnw_ref[...]).astype(BF16)
    acc = None
    for c in range(D_FF // FFN_FC):
        a = _dot(h, win_ref[:, c * FFN_FC:(c + 1) * FFN_FC])
        b = _dot(h, win_ref[:, D_FF + c * FFN_FC:D_FF + (c + 1) * FFN_FC])
        p = _dot((_silu(a) * b).astype(BF16), wout_ref[c * FFN_FC:(c + 1) * FFN_FC, :])
        acc = p if acc is None else acc + p
    y = x + 0.5 * acc
    if final_norm:
        y = _rms(y, lnf_ref[...])
    o_ref[...] = y
    if emit_mix:
        hm_ref, sm_ref = rest
        hm = _rms(y, lnm_ref[...]).astype(BF16)
        hm_ref[...] = hm
        sm_ref[...] = _dot(hm, wsm_ref[...])


def _ffn(x, lnw, w_in, w_out, lnf=None, mix=None):
    T = x.shape[0]
    tok = pl.BlockSpec((FFN_TM, D_MODEL), lambda i: (i, 0))
    in_specs = [tok, _resident((1, D_MODEL)), _resident((D_MODEL, 2 * D_FF)), _resident((D_FF, D_MODEL))]
    args = [x, lnw, w_in, w_out]
    out_specs = [tok]
    out_shape = [jax.ShapeDtypeStruct((T, D_MODEL), F32)]
    if lnf is not None:
        in_specs.append(_resident((1, D_MODEL)))
        args.append(lnf)
    if mix is not None:
        in_specs += [_resident((1, D_MODEL)), _resident((D_MODEL, SMALL_W))]
        args += list(mix)
        out_specs += [tok, pl.BlockSpec((FFN_TM, SMALL_W), lambda i: (i, 0))]
        out_shape += [jax.ShapeDtypeStruct((T, D_MODEL), BF16), jax.ShapeDtypeStruct((T, SMALL_W), F32)]
    out = pl.pallas_call(
        functools.partial(_ffn_body, final_norm=lnf is not None, emit_mix=mix is not None),
        grid=(T // FFN_TM,),
        in_specs=in_specs,
        out_specs=out_specs,
        out_shape=out_shape,
        compiler_params=_cparams(("parallel",), 56),
        name="ffn",
    )(*args)
    return out if mix is not None else out[0]


def _inproj_body(h_ref, w_ref, u_ref):
    u_ref[...] = _dot(h_ref[...], w_ref[...]).astype(BF16)


def _inproj(h, w):
    T = h.shape[0]
    return pl.pallas_call(
        _inproj_body,
        grid=(U_WIDTH // U_TN, T // PROJ_TM),
        in_specs=[pl.BlockSpec((PROJ_TM, D_MODEL), lambda j, i: (i, 0)),
                  pl.BlockSpec((D_MODEL, U_TN), lambda j, i: (0, j))],
        out_specs=pl.BlockSpec((PROJ_TM, U_TN), lambda j, i: (i, j)),
        out_shape=jax.ShapeDtypeStruct((T, U_WIDTH), BF16),
        compiler_params=_cparams(("arbitrary", "arbitrary"), 48),
        name="inproj",
    )(h, w)


def _ssd_chunk(x_get, Bc, Cc, sm, dtb, alog, S_ref, reverse):
    Q = SSM_CHUNK
    off = SSM_HEADS if reverse else 0
    last = 0 if reverse else Q - 1
    ii = lax.broadcasted_iota(jnp.int32, (Q, Q), 0)
    jj = lax.broadcasted_iota(jnp.int32, (Q, Q), 1)
    tri = (ii <= jj) if reverse else (ii >= jj)
    lo = jj < SSM_HEAD_DIM
    dt = jax.nn.softplus(sm + dtb)
    dA = dt * (-jnp.exp(alog))
    cs = _cumsum_dot(tri.astype(BF16), dA)
    csT = cs.T
    dtT = dt.T
    ecs = jnp.exp(cs)
    WT = jnp.exp(csT[:, last:last + 1] - csT) * dtT
    el = ecs[last:last + 1, :]
    ys = []
    for g in range(SSM_GROUPS):
        Bg = Bc[:, g * SSM_STATE:(g + 1) * SSM_STATE]
        Cg = Cc[:, g * SSM_STATE:(g + 1) * SSM_STATE]
        cb = _dot_nt(Cg.astype(BF16), Bg.astype(BF16))
        BgT = Bg.T
        for pp in range(SSM_HEADS // SSM_GROUPS // 2):
            hp = g * (SSM_HEADS // SSM_GROUPS // 2) + pp
            xb = x_get(hp)
            Sp = S_ref[hp]
            rhs = jnp.concatenate([xb, Sp.astype(BF16)], axis=0)
            outs, dSs, scs = [], [], []
            for e in range(2):
                h = off + 2 * hp + e
                seg = cs[:, h:h + 1] - csT[h:h + 1, :]
                Lm = jnp.where(tri, jnp.exp(seg), 0.0)
                M = (cb * Lm * dtT[h:h + 1, :]).astype(BF16)
                Ce = (Cg * ecs[:, h:h + 1]).astype(BF16)
                outs.append(_dot(jnp.concatenate([M, Ce], axis=1), rhs))
                dSs.append(_dot((BgT * WT[h:h + 1, :]).astype(BF16), xb))
                scs.append(el[:, h:h + 1])
            ys.append(jnp.where(lo, outs[0], outs[1]))
            S_ref[hp] = Sp * jnp.where(lo, scs[0], scs[1]) + jnp.where(lo, dSs[0], dSs[1])
    return ys


def _conv_silu_rows(ext_ref, w_ref, b_ref, r0, c0, cw):
    acc = b_ref[:, c0:c0 + cw]
    for k in range(SSM_CONV):
        s = BF16_ROWS - SSM_CONV // 2 + k + r0
        acc = acc + ext_ref[s:s + SSM_CHUNK, c0:c0 + cw] * w_ref[k:k + 1, c0:c0 + cw]
    return _silu(acc)


def _ssd_fwd_body(xs_ref, xsp_ref, xsn_ref, bc_ref, bcp_ref, bcn_ref, sm_ref,
                  cwx_ref, cbx_ref, cwb_ref, cbb_ref, dtb_ref, alog_ref,
                  yf_ref, xa_ref, ba_ref, extx, extb, S_ref, *, nblk):
    c = pl.program_id(1)
    TQ = SSD_TQ
    H = BF16_ROWS

    @pl.when(c == 0)
    def _():
        S_ref[...] = jnp.zeros_like(S_ref)

    first = c == 0
    final = c == nblk - 1
    for ext, cur, prv, nxt in ((extx, xs_ref, xsp_ref, xsn_ref), (extb, bc_ref, bcp_ref, bcn_ref)):
        ext[0:H, :] = jnp.where(first, 0.0, prv[...].astype(F32))
        ext[H:H + TQ, :] = cur[...].astype(F32)
        ext[H + TQ:2 * H + TQ, :] = jnp.where(final, 0.0, nxt[...].astype(F32))

    for q in range(TQ // SSM_CHUNK):
        r0 = q * SSM_CHUNK
        rows = slice(r0, r0 + SSM_CHUNK)
        for c0 in range(0, SSM_WIDTH, 256):
            xa_ref[rows, c0:c0 + 256] = _conv_silu_rows(extx, cwx_ref, cbx_ref, r0, c0, 256).astype(BF16)
        for c0 in range(0, 512, 256):
            ba_ref[rows, c0:c0 + 256] = _conv_silu_rows(extb, cwb_ref, cbb_ref, r0, c0, 256).astype(BF16)
        ys = _ssd_chunk(lambda hp: xa_ref[rows, hp * LANES:(hp + 1) * LANES],
                        ba_ref[rows, 0:256].astype(F32), ba_ref[rows, 256:512].astype(F32), sm_ref[rows, :],
                        dtb_ref[...], alog_ref[...], S_ref, reverse=False)
        for hp, y in enumerate(ys):
            yf_ref[rows, hp * LANES:(hp + 1) * LANES] = y.astype(BF16)


def _ssd_bwd_body(xa_ref, ba_ref, sm_ref, z_ref, yf_ref, dtb_ref, alog_ref, dsk_ref, nw_ref,
                  ya_ref, S_ref):
    c = pl.program_id(1)
    TQ = SSD_TQ

    @pl.when(c == 0)
    def _():
        S_ref[...] = jnp.zeros_like(S_ref)

    half = SSM_WIDTH // SSM_GROUPS
    for q in reversed(range(TQ // SSM_CHUNK)):
        rows = slice(q * SSM_CHUNK, (q + 1) * SSM_CHUNK)
        ys = _ssd_chunk(lambda hp: xa_ref[rows, hp * LANES:(hp + 1) * LANES],
                        ba_ref[rows, 0:256].astype(F32), ba_ref[rows, 256:512].astype(F32), sm_ref[rows, :],
                        dtb_ref[...], alog_ref[...], S_ref, reverse=True)
        for g in range(SSM_GROUPS):
            ts = []
            ssq = None
            for pp in range(4):
                hp = g * 4 + pp
                cols = slice(hp * LANES, (hp + 1) * LANES)
                t = ys[hp] + yf_ref[rows, cols].astype(F32) + dsk_ref[:, cols] * xa_ref[rows, cols].astype(F32)
                t = t * _silu(z_ref[rows, cols].astype(F32))
                ts.append(t)
                s = jnp.sum(t * t, axis=-1, keepdims=True)
                ssq = s if ssq is None else ssq + s
            inv = lax.rsqrt(ssq * (1.0 / half) + EPS)
            for pp in range(4):
                cols = slice((g * 4 + pp) * LANES, (g * 4 + pp + 1) * LANES)
                ya_ref[rows, cols] = (ts[pp] * inv * nw_ref[:, cols]).astype(BF16)


def _ssd(u, sm, B, L, p):
    T = B * L
    TQ = SSD_TQ
    nblk = L // TQ
    hb = TQ // BF16_ROWS
    nh = T // BF16_ROWS

    def rowf(b, c):
        return b * nblk + c

    def rowb(b, c):
        return b * nblk + (nblk - 1 - c)

    def col(width, colstart, rowfn):
        return pl.BlockSpec((TQ, width), lambda b, c: (rowfn(b, c), colstart // width))

    def halo(width, colstart, shift):
        def im(b, c):
            r = (b * nblk + c + shift) * hb - (1 - shift)
            return (jnp.clip(r, 0, nh - 1), colstart // width)
        return pl.BlockSpec((BF16_ROWS, width), im)

    state = pltpu.VMEM((SSM_HEADS // 2, SSM_STATE, LANES), F32)
    yf, xa, ba = pl.pallas_call(
        functools.partial(_ssd_fwd_body, nblk=nblk),
        grid=(B, nblk),
        in_specs=[col(1024, U_XS, rowf), halo(1024, U_XS, 0), halo(1024, U_XS, 1),
                  col(512, U_BC, rowf), halo(512, U_BC, 0), halo(512, U_BC, 1),
                  col(SMALL_W, 0, rowf),
                  _resident((SUBLANES, 1024)), _resident((1, 1024)),
                  _resident((SUBLANES, 512)), _resident((1, 512)),
                  _resident((1, LANES)), _resident((1, LANES))],
        out_specs=[pl.BlockSpec((TQ, 1024), lambda b, c: (rowf(b, c), 0)),
                   pl.BlockSpec((TQ, 1024), lambda b, c: (rowf(b, c), 0)),
                   pl.BlockSpec((TQ, 512), lambda b, c: (rowf(b, c), 0))],
        out_shape=[jax.ShapeDtypeStruct((T, 1024), BF16), jax.ShapeDtypeStruct((T, 1024), BF16),
                   jax.ShapeDtypeStruct((T, 512), BF16)],
        scratch_shapes=[pltpu.VMEM((TQ + 2 * BF16_ROWS, 1024), F32),
                        pltpu.VMEM((TQ + 2 * BF16_ROWS, 512), F32), state],
        compiler_params=_cparams(("arbitrary", "arbitrary"), 40),
        name="ssd_fwd",
    )(u, u, u, u, u, u, sm, p["cwx"], p["cbx"], p["cwb"], p["cbb"], p["dtb"], p["alog"])

    def rb(width):
        return pl.BlockSpec((TQ, width), lambda b, c: (rowb(b, c), 0))

    return pl.pallas_call(
        _ssd_bwd_body,
        grid=(B, nblk),
        in_specs=[rb(1024), rb(512), col(SMALL_W, 0, rowb), col(1024, U_Z, rowb), rb(1024),
                  _resident((1, LANES)), _resident((1, LANES)),
                  _resident((1, 1024)), _resident((1, 1024))],
        out_specs=rb(1024),
        out_shape=jax.ShapeDtypeStruct((T, 1024), BF16),
        scratch_shapes=[state],
        compiler_params=_cparams(("arbitrary", "arbitrary"), 40),
        name="ssd_bwd",
    )(xa, ba, sm, u, yf, p["dtb"], p["alog"], p["dskip"], p["ssm_nw"])


def _gla_block(qk_ref, v_ref, sm_ref, gup_ref, gb_ref, S_ref, qe_s, ke_s, kd_s, att_s, dS_s, Sb_s, reverse, emit):
    Q = GLA_CHUNK
    nch = GLA_TQ // Q
    last = 0 if reverse else Q - 1
    ii = lax.broadcasted_iota(jnp.int32, (Q, Q), 0)
    jj = lax.broadcasted_iota(jnp.int32, (Q, Q), 1)
    tri = (ii <= jj) if reverse else (ii >= jj)
    tri_bf = tri.astype(BF16)
    rows = [slice(c * Q, (c + 1) * Q) for c in range(nch)]
    heads = [slice(h * GLA_DK_HEAD, (h + 1) * GLA_DK_HEAD) for h in range(GLA_HEADS)]
    vcols = [slice(h * GLA_DV_HEAD, (h + 1) * GLA_DV_HEAD) for h in range(GLA_HEADS)]

    logits = [_dot(sm_ref[r, :].astype(BF16), gup_ref[...]) for r in rows]
    parts = [_split3(jax.nn.log_sigmoid(lg + gb_ref[...]) / GLA_NORMALIZER) for lg in logits]
    bcums = [_dot(tri_bf, hi) + _dot(tri_bf, mid) + _dot(tri_bf, lo) for hi, mid, lo in parts]
    bls = []
    for r, bcum in zip(rows, bcums):
        bl = bcum[last:last + 1, :]
        q = qk_ref[r, 0:GLA_DK].astype(F32)
        k = qk_ref[r, GLA_DK:2 * GLA_DK].astype(F32)
        qe_s[r, :] = ((q * (GLA_DK_HEAD ** -0.5)) * jnp.exp(bcum)).astype(BF16)
        ke_s[r, :] = (k * jnp.exp(-bcum)).astype(BF16)
        kd_s[r, :] = (k * jnp.exp(bl - bcum)).astype(BF16)
        bls.append(bl)

    for c, r in enumerate(rows):
        for h in range(GLA_HEADS):
            att = _dot_nt(qe_s[r, heads[h]], ke_s[r, heads[h]])
            att_s[c * GLA_HEADS + h] = jnp.where(tri, att, 0.0).astype(BF16)
    for c, r in enumerate(rows):
        for h in range(GLA_HEADS):
            dS_s[c * GLA_HEADS + h] = _dot_tn(kd_s[r, heads[h]], v_ref[r, vcols[h]])

    order = list(reversed(range(nch))) if reverse else list(range(nch))
    for h in range(GLA_HEADS):
        S = S_ref[h]
        for c in order:
            Sb_s[c * GLA_HEADS + h] = S.astype(BF16)
            d = jnp.exp(bls[c][:, heads[h]])
            dcol = jnp.broadcast_to(d, (GLA_DK_HEAD, GLA_DK_HEAD)).T
            S = S * jnp.concatenate([dcol, dcol], axis=1) + dS_s[c * GLA_HEADS + h]
        S_ref[h] = S

    for c, r in enumerate(rows):
        for h in range(GLA_HEADS):
            o = _dot(att_s[c * GLA_HEADS + h], v_ref[r, vcols[h]]) + _dot(qe_s[r, heads[h]], Sb_s[c * GLA_HEADS + h])
            emit(r, vcols[h], o)


def _gla_fwd_body(qk_ref, v_ref, sm_ref, gup_ref, gb_ref, of_ref, S_ref, *scratch):
    @pl.when(pl.program_id(1) == 0)
    def _():
        S_ref[...] = jnp.zeros_like(S_ref)

    def emit(r, cols, o):
        of_ref[r, cols] = o.astype(BF16)

    _gla_block(qk_ref, v_ref, sm_ref, gup_ref, gb_ref, S_ref, *scratch, reverse=False, emit=emit)


def _gla_bwd_body(qk_ref, v_ref, sm_ref, g_ref, of_ref, gup_ref, gb_ref, nw_ref, yb_ref, S_ref, *scratch):
    @pl.when(pl.program_id(1) == 0)
    def _():
        S_ref[...] = jnp.zeros_like(S_ref)

    def emit(r, cols, ob):
        o = ob + of_ref[r, cols].astype(F32)
        o = o * lax.rsqrt(jnp.mean(o * o, axis=-1, keepdims=True) + EPS) * nw_ref[...]
        yb_ref[r, cols] = (o * _silu(g_ref[r, cols].astype(F32))).astype(BF16)

    _gla_block(qk_ref, v_ref, sm_ref, gup_ref, gb_ref, S_ref, *scratch, reverse=True, emit=emit)


def _gla(u, sm, B, L, p):
    T = B * L
    TQ = GLA_TQ
    nblk = L // TQ
    nunits = (TQ // GLA_CHUNK) * GLA_HEADS

    def rowf(b, c):
        return b * nblk + c

    def rowb(b, c):
        return b * nblk + (nblk - 1 - c)

    def col(width, colstart, rowfn):
        return pl.BlockSpec((TQ, width), lambda b, c: (rowfn(b, c), colstart // width))

    scratch = [pltpu.VMEM((GLA_HEADS, GLA_DK_HEAD, GLA_DV_HEAD), F32),
               pltpu.VMEM((TQ, GLA_DK), BF16), pltpu.VMEM((TQ, GLA_DK), BF16), pltpu.VMEM((TQ, GLA_DK), BF16),
               pltpu.VMEM((nunits, GLA_CHUNK, GLA_CHUNK), BF16),
               pltpu.VMEM((nunits, GLA_DK_HEAD, GLA_DV_HEAD), F32),
               pltpu.VMEM((nunits, GLA_DK_HEAD, GLA_DV_HEAD), BF16)]
    of = pl.pallas_call(
        _gla_fwd_body,
        grid=(B, nblk),
        in_specs=[col(1024, U_QK, rowf), col(1024, U_V, rowf), col(SMALL_W, 0, rowf),
                  _resident((LANES, GLA_DK)), _resident((1, GLA_DK))],
        out_specs=pl.BlockSpec((TQ, 1024), lambda b, c: (rowf(b, c), 0)),
        out_shape=jax.ShapeDtypeStruct((T, GLA_DV), BF16),
        scratch_shapes=scratch,
        compiler_params=_cparams(("arbitrary", "arbitrary"), 40),
        name="gla_fwd",
    )(u, u, sm, p["gup_f"], p["gb_f"])
    return pl.pallas_call(
        _gla_bwd_body,
        grid=(B, nblk),
        in_specs=[col(1024, U_QK, rowb), col(1024, U_V, rowb), col(SMALL_W, 0, rowb),
                  col(1024, U_G, rowb), pl.BlockSpec((TQ, 1024), lambda b, c: (rowb(b, c), 0)),
                  _resident((LANES, GLA_DK)), _resident((1, GLA_DK)), _resident((1, GLA_DV_HEAD))],
        out_specs=pl.BlockSpec((TQ, 1024), lambda b, c: (rowb(b, c), 0)),
        out_shape=jax.ShapeDtypeStruct((T, GLA_DV), BF16),
        scratch_shapes=scratch,
        compiler_params=_cparams(("arbitrary", "arbitrary"), 40),
        name="gla_bwd",
    )(u, u, sm, u, of, p["gup_b"], p["gb_b"], p["gla_nw"])


def _na_body(q_ref, kp_ref, kc_ref, kn_ref, vp_ref, vc_ref, vn_ref, tab_ref, o_ref, kbuf, vbuf, s_s, p_s, *, nblk):
    i = pl.program_id(1)
    TQ = NA_TQ
    rows_total = nblk * NA_ROWS
    npairs = NA_HEADS // 2
    for buf, refs in ((kbuf, (kp_ref, kc_ref, kn_ref)), (vbuf, (vp_ref, vc_ref, vn_ref))):
        for n, r in enumerate(refs):
            buf[n * TQ:(n + 1) * TQ, :] = r[...]

    lane = lax.broadcasted_iota(jnp.int32, (GRID_W, LANES), 1)
    lo = lane < NA_HEAD_DIM
    nkeys = NA_KH * GRID_W

    def row_body(rho, carry):
        r = i * NA_ROWS + rho
        rs = jnp.clip(r - NA_KH // 2, 0, rows_total - NA_KH)
        delta = r - rs
        keys = pl.ds(pl.multiple_of((rs - (i - 1) * NA_ROWS) * GRID_W, GRID_W), nkeys)
        qrows = pl.ds(pl.multiple_of(rho * GRID_W, GRID_W), GRID_W)
        cols = [slice(hp * LANES, (hp + 1) * LANES) for hp in range(npairs)]
        ms = []
        for hp in range(npairs):
            qp = q_ref[qrows, cols[hp]] * (NA_HEAD_DIM ** -0.5)
            zero = jnp.zeros_like(qp)
            lhs = jnp.concatenate([jnp.where(lo, qp, zero), jnp.where(lo, zero, qp)], axis=0)
            bias = jnp.concatenate(
                [tab_ref[2 * jp - delta + NA_KH - 1, hp] for jp in range(NA_KH // 2)], axis=1)
            s = _dot_nt(lhs, kbuf[keys, cols[hp]]) + bias
            s_s[hp] = s
            ms.append(jnp.max(s, axis=-1, keepdims=True))
        ls = []
        for hp in range(npairs):
            pexp = jnp.exp(s_s[hp] - ms[hp])
            ls.append(jnp.sum(pexp, axis=-1, keepdims=True))
            p_s[hp] = pexp.astype(BF16)
        for hp in range(npairs):
            o = _dot(p_s[hp], vbuf[keys, cols[hp]]) / ls[hp]
            o_ref[qrows, cols[hp]] = jnp.where(lo, o[0:GRID_W], o[GRID_W:2 * GRID_W]).astype(BF16)
        return carry

    lax.fori_loop(0, NA_ROWS, row_body, 0)


def _na(u, B, L, tab):
    T = B * L
    TQ = NA_TQ
    nblk = L // TQ
    npairs = NA_HEADS // 2

    def blk(colstart, shift):
        def im(b, i):
            return (b * nblk + jnp.clip(i + shift, 0, nblk - 1), colstart // 1024)
        return pl.BlockSpec((TQ, 1024), im)

    return pl.pallas_call(
        functools.partial(_na_body, nblk=nblk),
        grid=(B, nblk),
        in_specs=[blk(U_QC, 0), blk(U_KC, -1), blk(U_KC, 0), blk(U_KC, 1),
                  blk(U_VC, -1), blk(U_VC, 0), blk(U_VC, 1),
                  _resident(tab.shape)],
        out_specs=pl.BlockSpec((TQ, 1024), lambda b, i: (b * nblk + i, 0)),
        out_shape=jax.ShapeDtypeStruct((T, 1024), BF16),
        scratch_shapes=[pltpu.VMEM((3 * TQ, 1024), BF16), pltpu.VMEM((3 * TQ, 1024), BF16),
                        pltpu.VMEM((npairs, 2 * GRID_W, NA_KH * GRID_W), F32),
                        pltpu.VMEM((npairs, 2 * GRID_W, NA_KH * GRID_W), BF16)],
        compiler_params=_cparams(("parallel", "parallel"), 48),
        name="na",
    )(u, u, u, u, u, u, u, tab)


def _na_bias_table(rpb):
    c = jnp.arange(GRID_W)[:, None]
    kc = jnp.arange(GRID_W)[None, :]
    cs = jnp.clip(c - NA_KW // 2, 0, GRID_W - NA_KW)
    valid = (kc >= cs) & (kc < cs + NA_KW)
    off = jnp.clip(kc - c + NA_KW - 1, 0, 2 * NA_KW - 2)
    t = jnp.where(valid, rpb.astype(F32)[:, :, off], MASK_NEG)
    tp = jnp.concatenate([t[:, :-1], t[:, 1:]], axis=-1)
    n = 2 * NA_KH - 2
    tp = tp.reshape(NA_HEADS // 2, 2, n, GRID_W, LANES).transpose(2, 0, 1, 3, 4)
    return tp.reshape(n, NA_HEADS // 2, 2 * GRID_W, LANES)


def _merge_body(x_ref, ya_ref, yb_ref, yc_ref, g0_ref, g1_ref, g2_ref, wa_ref, wb_ref, wc_ref, wo_ref, o_ref):
    def sig(g_ref):
        return jax.nn.sigmoid(g_ref[...].astype(F32))

    m = sig(g0_ref) * _dot(ya_ref[...], wa_ref[...])
    m = m + sig(g1_ref) * _dot(yb_ref[...], wb_ref[...])
    m = m + sig(g2_ref) * _dot(yc_ref[...], wc_ref[...])
    o_ref[...] = x_ref[...] + _dot(m.astype(BF16), wo_ref[...])


def _merge(x, ya, yb, yc, u, wa, wb, wc, wo):
    T = x.shape[0]
    tok = pl.BlockSpec((MERGE_TM, D_MODEL), lambda i: (i, 0))

    def gate(k):
        return pl.BlockSpec((MERGE_TM, D_MODEL), lambda i: (i, U_G0 // 1024 + k))

    w = _resident((D_MODEL, D_MODEL))
    return pl.pallas_call(
        _merge_body,
        grid=(T // MERGE_TM,),
        in_specs=[tok, tok, tok, tok, gate(0), gate(1), gate(2), w, w, w, w],
        out_specs=tok,
        out_shape=jax.ShapeDtypeStruct((T, D_MODEL), F32),
        compiler_params=_cparams(("parallel",), 40),
        name="merge",
    )(x, ya, yb, yc, u, u, u, wa, wb, wc, wo)


def _pad_rows(w, rows):
    return jnp.pad(w, ((0, rows - w.shape[0]), (0, 0)))


def _layer_params(i, ln_ffn1_w, ffn1_w_in, ffn1_w_out, ln_mix_w, w_in, ssm_conv_w, ssm_conv_b, ssm_dt_bias,
                  ssm_a_log, ssm_d, ssm_norm_w, gla_gate_up, gla_gate_b, gla_norm_w, na_rpb,
                  w_branch_a, w_branch_b, w_branch_c, w_out, ln_ffn2_w, ffn2_w_in, ffn2_w_out):
    w = w_in[i]
    o = 0
    parts = {}
    for name, width in (("z", 1024), ("xs", 1024), ("bc", 512), ("dt", 32), ("q", 512), ("k", 512),
                        ("v", 1024), ("g", 1024), ("dn", 32), ("qc", 1024), ("kc", 1024), ("vc", 1024),
                        ("gates", 3072)):
        parts[name] = w[:, o:o + width]
        o += width
    w_small = jnp.pad(jnp.concatenate([parts["dt"], parts["dn"]], axis=1), ((0, 0), (0, SMALL_W - 64))).astype(BF16)
    w_perm = jnp.concatenate([parts[n] for n in ("z", "xs", "q", "k", "v", "g", "qc", "kc", "vc", "gates", "bc")],
                             axis=1).astype(BF16)

    def gup(d):
        full = jnp.zeros((LANES, GLA_DK), F32).at[32 + 16 * d:48 + 16 * d].set(gla_gate_up[i, d].astype(F32))
        return full.astype(BF16)

    def pad_lanes(v2):
        return jnp.pad(v2.astype(F32).reshape(1, 2 * SSM_HEADS), ((0, 0), (0, LANES - 2 * SSM_HEADS)))

    return dict(
        ln1=ln_ffn1_w[i].reshape(1, -1), f1_in=ffn1_w_in[i].astype(BF16), f1_out=ffn1_w_out[i].astype(BF16),
        ln_mix=ln_mix_w[i].reshape(1, -1), w_perm=w_perm, w_small=w_small,
        cwx=_pad_rows(ssm_conv_w[i][:, :1024], SUBLANES), cbx=ssm_conv_b[i][:1024].reshape(1, -1),
        cwb=_pad_rows(ssm_conv_w[i][:, 1024:], SUBLANES), cbb=ssm_conv_b[i][1024:].reshape(1, -1),
        dtb=pad_lanes(ssm_dt_bias[i]), alog=pad_lanes(ssm_a_log[i]),
        dskip=jnp.repeat(ssm_d[i].astype(F32), SSM_HEAD_DIM).reshape(1, -1),
        ssm_nw=ssm_norm_w[i].reshape(1, -1),
        gup_f=gup(0), gup_b=gup(1), gb_f=gla_gate_b[i, 0].reshape(1, -1), gb_b=gla_gate_b[i, 1].reshape(1, -1),
        gla_nw=gla_norm_w[i].reshape(1, -1),
        na_tab=_na_bias_table(na_rpb[i]),
        wa=w_branch_a[i].astype(BF16), wb=w_branch_b[i].astype(BF16), wc=w_branch_c[i].astype(BF16),
        wo=w_out[i].astype(BF16),
        ln2=ln_ffn2_w[i].reshape(1, -1), f2_in=ffn2_w_in[i].astype(BF16), f2_out=ffn2_w_out[i].astype(BF16),
    )


def _trunk(x3, layers, ln_final):
    B, L, D = x3.shape
    x = x3.reshape(B * L, D)
    for li, p in enumerate(layers):
        x, h, sm = _ffn(x, p["ln1"], p["f1_in"], p["f1_out"], mix=(p["ln_mix"], p["w_small"]))
        u = _inproj(h, p["w_perm"])
        ya = _ssd(u, sm, B, L, p)
        yb = _gla(u, sm, B, L, p)
        yc = _na(u, B, L, p["na_tab"])
        x = _merge(x, ya, yb, yc, u, p["wa"], p["wb"], p["wc"], p["wo"])
        x = _ffn(x, p["ln2"], p["f2_in"], p["f2_out"], lnf=ln_final if li == len(layers) - 1 else None)
    return x.reshape(B, L, D)


def kernel(x_prompt, x_sample, ln_ffn1_w, ffn1_w_in, ffn1_w_out, ln_mix_w, w_in, ssm_conv_w, ssm_conv_b, ssm_dt_bias, ssm_a_log, ssm_d, ssm_norm_w, gla_gate_up, gla_gate_b, gla_norm_w, na_rpb, w_branch_a, w_branch_b, w_branch_c, w_out, ln_ffn2_w, ffn2_w_in, ffn2_w_out, ln_final_w):
    layers = [_layer_params(i, ln_ffn1_w, ffn1_w_in, ffn1_w_out, ln_mix_w, w_in, ssm_conv_w, ssm_conv_b,
                            ssm_dt_bias, ssm_a_log, ssm_d, ssm_norm_w, gla_gate_up, gla_gate_b, gla_norm_w,
                            na_rpb, w_branch_a, w_branch_b, w_branch_c, w_out, ln_ffn2_w, ffn2_w_in, ffn2_w_out)
              for i in range(DEPTH)]
    ln_final = ln_final_w.reshape(1, -1)
    return (_trunk(x_prompt, layers, ln_final), _trunk(x_sample, layers, ln_final))
```

```python
import functools

import jax
import jax.numpy as jnp
from jax import lax
from jax.experimental import pallas as pl
from jax.experimental.pallas import tpu as pltpu

F32 = jnp.float32
BF16 = jnp.bfloat16

D_MODEL = 1024
DEPTH = 2
GRID_W = 64
SSM_HEADS = 16
SSM_HEAD_DIM = 64
SSM_WIDTH = 1024
SSM_GROUPS = 2
SSM_STATE = 128
SSM_CONV = 5
SSM_CHUNK = 128
GLA_HEADS = 4
GLA_DK_HEAD = 128
GLA_DV_HEAD = 256
GLA_DK = 512
GLA_DV = 1024
GLA_RANK = 16
GLA_NORMALIZER = 16.0
GLA_CHUNK = 64
NA_HEADS = 16
NA_HEAD_DIM = 64
NA_KH = 8
NA_KW = 16
D_FF = 2816
EPS = 1e-6

LANES = 128
SUBLANES = 8
BF16_ROWS = 16
MASK_NEG = -1e30
LOG2E = 1.4426950408889634

U_Z, U_XS, U_QK, U_V, U_G, U_QC, U_KC, U_VC, U_G0 = (i * 1024 for i in range(9))
U_BC = 11 * 1024
U_WIDTH = U_BC + 512
U_TN = U_WIDTH // 2
SMALL_W = LANES

MXU_N = 256
FFN_TM = 1024
FFN_CHUNKS = tuple((c, min(c + 3 * MXU_N, D_FF)) for c in range(0, D_FF, 3 * MXU_N))
PROJ_TM = 512
SSD_TQ = 512
GLA_TQ = 512
NA_ROWS = 8
NA_TQ = NA_ROWS * GRID_W
NA_RPT = 2
MERGE_TM = 512


def _cparams(sem, vmem_mb):
    return pltpu.CompilerParams(dimension_semantics=sem, vmem_limit_bytes=vmem_mb * 1024 * 1024)


def _resident(shape):
    nd = len(shape)
    return pl.BlockSpec(shape, lambda *_: (0,) * nd, pipeline_mode=pl.Buffered(1))


def _rms(x, w):
    return x * lax.rsqrt(jnp.mean(x * x, axis=-1, keepdims=True) + EPS) * w


def _silu(x):
    return x * jax.nn.sigmoid(x)


def _dot(a, b):
    return jnp.dot(a, b, preferred_element_type=F32)


def _dot_nt(a, b):
    return lax.dot_general(a, b, (((1,), (1,)), ((), ())), preferred_element_type=F32)


def _dot_tn(a, b):
    return lax.dot_general(a, b, (((0,), (0,)), ((), ())), preferred_element_type=F32)


def _split3(x):
    hi = x.astype(BF16)
    r1 = x - hi.astype(F32)
    mid = r1.astype(BF16)
    lo = (r1 - mid.astype(F32)).astype(BF16)
    return hi, mid, lo


def _lane_cumsum(x, reverse):
    n = x.shape[-1]
    lane = lax.broadcasted_iota(jnp.int32, x.shape, x.ndim - 1)
    s = 1
    while s < n:
        if reverse:
            x = x + jnp.where(lane < n - s, pltpu.roll(x, n - s, x.ndim - 1), 0.0)
        else:
            x = x + jnp.where(lane >= s, pltpu.roll(x, s, x.ndim - 1), 0.0)
        s *= 2
    return x


def _cumsum_dot(tri_bf, x):
    hi, mid, lo = _split3(x)
    return _dot(tri_bf, hi) + _dot(tri_bf, mid) + _dot(tri_bf, lo)


def _ffn_body(x_ref, lnw_ref, win_ref, wout_ref, *rest, final_norm, emit_mix):
    rest = list(rest)
    lnf_ref = rest.pop(0) if final_norm else None
    if emit_mix:
        lnm_ref, wsm_ref = rest.pop(0), rest.pop(0)
    o_ref = rest.pop(0)
    x = x_ref[...]
    h = _rms(x, lnw_ref[...]).astype(BF16)
    acc = None
    for c0, c1 in FFN_CHUNKS:
        a = _dot(h, win_ref[:, c0:c1])
        b = _dot(h, win_ref[:, D_FF + c0:D_FF + c1])
        p = _dot((_silu(a) * b).astype(BF16), wout_ref[c0:c1, :])
        acc = p if acc is None else acc + p
    y = x + 0.5 * acc
    if final_norm:
        y = _rms(y, lnf_ref[...])
    o_ref[...] = y
    if emit_mix:
        hm_ref, sm_ref = rest
        hm = _rms(y, lnm_ref[...]).astype(BF16)
        hm_ref[...] = hm
        sm_ref[...] = _dot(hm, wsm_ref[...])


def _ffn(x, lnw, w_in, w_out, lnf=None, mix=None):
    T = x.shape[0]
    tok = pl.BlockSpec((FFN_TM, D_MODEL), lambda i: (i, 0))
    in_specs = [tok, _resident((1, D_MODEL)), _resident((D_MODEL, 2 * D_FF)), _resident((D_FF, D_MODEL))]
    args = [x, lnw, w_in, w_out]
    out_specs = [tok]
    out_shape = [jax.ShapeDtypeStruct((T, D_MODEL), F32)]
    if lnf is not None:
        in_specs.append(_resident((1, D_MODEL)))
        args.append(lnf)
    if mix is not None:
        in_specs += [_resident((1, D_MODEL)), _resident((D_MODEL, SMALL_W))]
        args += list(mix)
        out_specs += [tok, pl.BlockSpec((FFN_TM, SMALL_W), lambda i: (i, 0))]
        out_shape += [jax.ShapeDtypeStruct((T, D_MODEL), BF16), jax.ShapeDtypeStruct((T, SMALL_W), F32)]
    out = pl.pallas_call(
        functools.partial(_ffn_body, final_norm=lnf is not None, emit_mix=mix is not None),
        grid=(T // FFN_TM,),
        in_specs=in_specs,
        out_specs=out_specs,
        out_shape=out_shape,
        compiler_params=_cparams(("parallel",), 56),
        name="ffn",
    )(*args)
    return out if mix is not None else out[0]


def _inproj_body(h_ref, w_ref, u_ref):
    u_ref[...] = _dot(h_ref[...], w_ref[...]).astype(BF16)


def _inproj(h, w):
    T = h.shape[0]
    return pl.pallas_call(
        _inproj_body,
        grid=(U_WIDTH // U_TN, T // PROJ_TM),
        in_specs=[pl.BlockSpec((PROJ_TM, D_MODEL), lambda j, i: (i, 0)),
                  pl.BlockSpec((D_MODEL, U_TN), lambda j, i: (0, j))],
        out_specs=pl.BlockSpec((PROJ_TM, U_TN), lambda j, i: (i, j)),
        out_shape=jax.ShapeDtypeStruct((T, U_WIDTH), BF16),
        compiler_params=_cparams(("arbitrary", "arbitrary"), 48),
        name="inproj",
    )(h, w)


def _ssd_decay_terms(sm_ref, dtb, alog, nch, reverse):
    Q = SSM_CHUNK
    off = SSM_HEADS if reverse else 0
    last = 0 if reverse else Q - 1
    raw = jnp.concatenate([sm_ref[c * Q:(c + 1) * Q, :].T[off:off + SSM_HEADS, :] for c in range(nch)], axis=0)
    tile = lambda p: jnp.concatenate([p[off:off + SSM_HEADS, :]] * nch, axis=0)
    dtT = jax.nn.softplus(raw + tile(dtb))
    kk = lax.broadcasted_iota(jnp.int32, (Q, Q), 0)
    jj = lax.broadcasted_iota(jnp.int32, (Q, Q), 1)
    triT = ((kk >= jj) if reverse else (kk <= jj)).astype(BF16)
    hi, mid, lo = _split3(dtT * (-jnp.exp(tile(alog))))
    csT = _dot(hi, triT) + _dot(mid, triT) + _dot(lo, triT)
    r2T = csT * LOG2E - jnp.log2(dtT)
    tot = csT[:, last:last + 1]
    WT = jnp.exp(tot - csT) * dtT
    elT = jnp.broadcast_to(jnp.exp(tot), csT.shape)
    cs = jnp.concatenate([csT, jnp.zeros((Q - nch * SSM_HEADS, Q), F32)], axis=0).T
    return r2T, WT, elT, cs * LOG2E, jnp.exp(cs)


def _ssd_chunk(x_get, Bc, Cc, terms, c, S_ref, reverse):
    Q = SSM_CHUNK
    r2T, WT, elT, c2, ecs = terms
    ii = lax.broadcasted_iota(jnp.int32, (Q, Q), 0)
    jj = lax.broadcasted_iota(jnp.int32, (Q, Q), 1)
    tri = (ii <= jj) if reverse else (ii >= jj)
    lo = jj < SSM_HEAD_DIM
    ys = []
    for g in range(SSM_GROUPS):
        Bg = Bc[:, g * SSM_STATE:(g + 1) * SSM_STATE]
        Cg = Cc[:, g * SSM_STATE:(g + 1) * SSM_STATE]
        cb = _dot_nt(Cg.astype(BF16), Bg.astype(BF16))
        BgT = Bg.T
        for pp in range(SSM_HEADS // SSM_GROUPS // 2):
            hp = g * (SSM_HEADS // SSM_GROUPS // 2) + pp
            xb = x_get(hp)
            Sp = S_ref[hp]
            rhs = jnp.concatenate([xb, Sp.astype(BF16)], axis=0)
            outs, dSs, scs = [], [], []
            for e in range(2):
                h = c * SSM_HEADS + 2 * hp + e
                Lm = jnp.where(tri, jnp.exp2(c2[:, h:h + 1] - r2T[h:h + 1, :]), 0.0)
                M = (cb * Lm).astype(BF16)
                Ce = (Cg * ecs[:, h:h + 1]).astype(BF16)
                outs.append(_dot(jnp.concatenate([M, Ce], axis=1), rhs))
                dSs.append(_dot((BgT * WT[h:h + 1, :]).astype(BF16), xb))
                scs.append(elT[h:h + 1, :])
            ys.append(jnp.where(lo, outs[0], outs[1]))
            S_ref[hp] = Sp * jnp.where(lo, scs[0], scs[1]) + jnp.where(lo, dSs[0], dSs[1])
    return ys


def _conv_silu_rows(ext_ref, w_ref, b_ref, r0, c0, cw):
    acc = b_ref[:, c0:c0 + cw]
    for k in range(SSM_CONV):
        s = BF16_ROWS - SSM_CONV // 2 + k + r0
        acc = acc + ext_ref[s:s + SSM_CHUNK, c0:c0 + cw] * w_ref[k:k + 1, c0:c0 + cw]
    return _silu(acc)


def _ssd_fwd_body(xs_ref, xsp_ref, xsn_ref, bc_ref, bcp_ref, bcn_ref, sm_ref,
                  cwx_ref, cbx_ref, cwb_ref, cbb_ref, dtb_ref, alog_ref,
                  yf_ref, xa_ref, ba_ref, extx, extb, S_ref, *, nblk):
    c = pl.program_id(1)
    TQ = SSD_TQ
    H = BF16_ROWS

    @pl.when(c == 0)
    def _():
        S_ref[...] = jnp.zeros_like(S_ref)

    first = c == 0
    final = c == nblk - 1
    for ext, cur, prv, nxt in ((extx, xs_ref, xsp_ref, xsn_ref), (extb, bc_ref, bcp_ref, bcn_ref)):
        ext[0:H, :] = jnp.where(first, 0.0, prv[...].astype(F32))
        ext[H:H + TQ, :] = cur[...].astype(F32)
        ext[H + TQ:2 * H + TQ, :] = jnp.where(final, 0.0, nxt[...].astype(F32))

    terms = _ssd_decay_terms(sm_ref, dtb_ref[...], alog_ref[...], TQ // SSM_CHUNK, reverse=False)
    for q in range(TQ // SSM_CHUNK):
        r0 = q * SSM_CHUNK
        rows = slice(r0, r0 + SSM_CHUNK)
        for c0 in range(0, SSM_WIDTH, 256):
            xa_ref[rows, c0:c0 + 256] = _conv_silu_rows(extx, cwx_ref, cbx_ref, r0, c0, 256).astype(BF16)
        for c0 in range(0, 512, 256):
            ba_ref[rows, c0:c0 + 256] = _conv_silu_rows(extb, cwb_ref, cbb_ref, r0, c0, 256).astype(BF16)
        ys = _ssd_chunk(lambda hp: xa_ref[rows, hp * LANES:(hp + 1) * LANES],
                        ba_ref[rows, 0:256].astype(F32), ba_ref[rows, 256:512].astype(F32),
                        terms, q, S_ref, reverse=False)
        for hp, y in enumerate(ys):
            yf_ref[rows, hp * LANES:(hp + 1) * LANES] = y.astype(BF16)


def _ssd_bwd_body(xa_ref, ba_ref, sm_ref, z_ref, yf_ref, dtb_ref, alog_ref, dsk_ref, nw_ref,
                  ya_ref, S_ref):
    c = pl.program_id(1)
    TQ = SSD_TQ

    @pl.when(c == 0)
    def _():
        S_ref[...] = jnp.zeros_like(S_ref)

    half = SSM_WIDTH // SSM_GROUPS
    terms = _ssd_decay_terms(sm_ref, dtb_ref[...], alog_ref[...], TQ // SSM_CHUNK, reverse=True)
    for q in reversed(range(TQ // SSM_CHUNK)):
        rows = slice(q * SSM_CHUNK, (q + 1) * SSM_CHUNK)
        ys = _ssd_chunk(lambda hp: xa_ref[rows, hp * LANES:(hp + 1) * LANES],
                        ba_ref[rows, 0:256].astype(F32), ba_ref[rows, 256:512].astype(F32),
                        terms, q, S_ref, reverse=True)
        for g in range(SSM_GROUPS):
            ts = []
            ssq = None
            for pp in range(4):
                hp = g * 4 + pp
                cols = slice(hp * LANES, (hp + 1) * LANES)
                t = ys[hp] + yf_ref[rows, cols].astype(F32) + dsk_ref[:, cols] * xa_ref[rows, cols].astype(F32)
                t = t * _silu(z_ref[rows, cols].astype(F32))
                ts.append(t)
                s = jnp.sum(t * t, axis=-1, keepdims=True)
                ssq = s if ssq is None else ssq + s
            inv = lax.rsqrt(ssq * (1.0 / half) + EPS)
            for pp in range(4):
                cols = slice((g * 4 + pp) * LANES, (g * 4 + pp + 1) * LANES)
                ya_ref[rows, cols] = (ts[pp] * inv * nw_ref[:, cols]).astype(BF16)


def _ssd(u, sm, B, L, p):
    T = B * L
    TQ = SSD_TQ
    nblk = L // TQ
    hb = TQ // BF16_ROWS
    nh = T // BF16_ROWS

    def rowf(b, c):
        return b * nblk + c

    def rowb(b, c):
        return b * nblk + (nblk - 1 - c)

    def col(width, colstart, rowfn):
        return pl.BlockSpec((TQ, width), lambda b, c: (rowfn(b, c), colstart // width))

    def halo(width, colstart, shift):
        def im(b, c):
            r = (b * nblk + c + shift) * hb - (1 - shift)
            return (jnp.clip(r, 0, nh - 1), colstart // width)
        return pl.BlockSpec((BF16_ROWS, width), im)

    state = pltpu.VMEM((SSM_HEADS // 2, SSM_STATE, LANES), F32)
    yf, xa, ba = pl.pallas_call(
        functools.partial(_ssd_fwd_body, nblk=nblk),
        grid=(B, nblk),
        in_specs=[col(1024, U_XS, rowf), halo(1024, U_XS, 0), halo(1024, U_XS, 1),
                  col(512, U_BC, rowf), halo(512, U_BC, 0), halo(512, U_BC, 1),
                  col(SMALL_W, 0, rowf),
                  _resident((SUBLANES, 1024)), _resident((1, 1024)),
                  _resident((SUBLANES, 512)), _resident((1, 512)),
                  _resident((2 * SSM_HEADS, LANES)), _resident((2 * SSM_HEADS, LANES))],
        out_specs=[pl.BlockSpec((TQ, 1024), lambda b, c: (rowf(b, c), 0)),
                   pl.BlockSpec((TQ, 1024), lambda b, c: (rowf(b, c), 0)),
                   pl.BlockSpec((TQ, 512), lambda b, c: (rowf(b, c), 0))],
        out_shape=[jax.ShapeDtypeStruct((T, 1024), BF16), jax.ShapeDtypeStruct((T, 1024), BF16),
                   jax.ShapeDtypeStruct((T, 512), BF16)],
        scratch_shapes=[pltpu.VMEM((TQ + 2 * BF16_ROWS, 1024), F32),
                        pltpu.VMEM((TQ + 2 * BF16_ROWS, 512), F32), state],
        compiler_params=_cparams(("arbitrary", "arbitrary"), 40),
        name="ssd_fwd",
    )(u, u, u, u, u, u, sm, p["cwx"], p["cbx"], p["cwb"], p["cbb"], p["dtb"], p["alog"])

    def rb(width):
        return pl.BlockSpec((TQ, width), lambda b, c: (rowb(b, c), 0))

    return pl.pallas_call(
        _ssd_bwd_body,
        grid=(B, nblk),
        in_specs=[rb(1024), rb(512), col(SMALL_W, 0, rowb), col(1024, U_Z, rowb), rb(1024),
                  _resident((2 * SSM_HEADS, LANES)), _resident((2 * SSM_HEADS, LANES)),
                  _resident((1, 1024)), _resident((1, 1024))],
        out_specs=rb(1024),
        out_shape=jax.ShapeDtypeStruct((T, 1024), BF16),
        scratch_shapes=[state],
        compiler_params=_cparams(("arbitrary", "arbitrary"), 40),
        name="ssd_bwd",
    )(xa, ba, sm, u, yf, p["dtb"], p["alog"], p["dskip"], p["ssm_nw"])


def _gla_block(qk_ref, v_ref, sm_ref, gup_ref, gb_ref, S_ref, qe_s, ke_s, kd_s, att_s, dS_s, Sb_s, reverse, emit):
    Q = GLA_CHUNK
    nch = GLA_TQ // Q
    last = 0 if reverse else Q - 1
    ii = lax.broadcasted_iota(jnp.int32, (Q, Q), 0)
    jj = lax.broadcasted_iota(jnp.int32, (Q, Q), 1)
    tri = (ii <= jj) if reverse else (ii >= jj)
    tri_bf = tri.astype(BF16)
    rows = [slice(c * Q, (c + 1) * Q) for c in range(nch)]
    heads = [slice(h * GLA_DK_HEAD, (h + 1) * GLA_DK_HEAD) for h in range(GLA_HEADS)]
    vcols = [slice(h * GLA_DV_HEAD, (h + 1) * GLA_DV_HEAD) for h in range(GLA_HEADS)]

    logits = [_dot(sm_ref[r, :].astype(BF16), gup_ref[...]) for r in rows]
    parts = [_split3(jax.nn.log_sigmoid(lg + gb_ref[...]) / GLA_NORMALIZER) for lg in logits]
    bcums = [_dot(tri_bf, hi) + _dot(tri_bf, mid) + _dot(tri_bf, lo) for hi, mid, lo in parts]
    bls = []
    for r, bcum in zip(rows, bcums):
        bl = bcum[last:last + 1, :]
        q = qk_ref[r, 0:GLA_DK].astype(F32)
        k = qk_ref[r, GLA_DK:2 * GLA_DK].astype(F32)
        qe_s[r, :] = ((q * (GLA_DK_HEAD ** -0.5)) * jnp.exp(bcum)).astype(BF16)
        ke_s[r, :] = (k * jnp.exp(-bcum)).astype(BF16)
        kd_s[r, :] = (k * jnp.exp(bl - bcum)).astype(BF16)
        bls.append(bl)

    for c, r in enumerate(rows):
        for h in range(GLA_HEADS):
            att = _dot_nt(qe_s[r, heads[h]], ke_s[r, heads[h]])
            att_s[c * GLA_HEADS + h] = jnp.where(tri, att, 0.0).astype(BF16)
    for c, r in enumerate(rows):
        for h in range(GLA_HEADS):
            dS_s[c * GLA_HEADS + h] = _dot_tn(kd_s[r, heads[h]], v_ref[r, vcols[h]])

    order = list(reversed(range(nch))) if reverse else list(range(nch))
    for h in range(GLA_HEADS):
        S = S_ref[h]
        for c in order:
            Sb_s[c * GLA_HEADS + h] = S.astype(BF16)
            d = jnp.exp(bls[c][:, heads[h]])
            dcol = jnp.broadcast_to(d, (GLA_DK_HEAD, GLA_DK_HEAD)).T
            S = S * jnp.concatenate([dcol, dcol], axis=1) + dS_s[c * GLA_HEADS + h]
        S_ref[h] = S

    for c, r in enumerate(rows):
        for h in range(GLA_HEADS):
            o = _dot(att_s[c * GLA_HEADS + h], v_ref[r, vcols[h]]) + _dot(qe_s[r, heads[h]], Sb_s[c * GLA_HEADS + h])
            emit(r, vcols[h], o)


def _gla_fwd_body(qk_ref, v_ref, sm_ref, gup_ref, gb_ref, of_ref, S_ref, *scratch):
    @pl.when(pl.program_id(1) == 0)
    def _():
        S_ref[...] = jnp.zeros_like(S_ref)

    def emit(r, cols, o):
        of_ref[r, cols] = o.astype(BF16)

    _gla_block(qk_ref, v_ref, sm_ref, gup_ref, gb_ref, S_ref, *scratch, reverse=False, emit=emit)


def _gla_bwd_body(qk_ref, v_ref, sm_ref, g_ref, of_ref, gup_ref, gb_ref, nw_ref, yb_ref, S_ref, *scratch):
    @pl.when(pl.program_id(1) == 0)
    def _():
        S_ref[...] = jnp.zeros_like(S_ref)

    def emit(r, cols, ob):
        o = ob + of_ref[r, cols].astype(F32)
        o = o * lax.rsqrt(jnp.mean(o * o, axis=-1, keepdims=True) + EPS) * nw_ref[...]
        yb_ref[r, cols] = (o * _silu(g_ref[r, cols].astype(F32))).astype(BF16)

    _gla_block(qk_ref, v_ref, sm_ref, gup_ref, gb_ref, S_ref, *scratch, reverse=True, emit=emit)


def _gla(u, sm, B, L, p):
    T = B * L
    TQ = GLA_TQ
    nblk = L // TQ
    nunits = (TQ // GLA_CHUNK) * GLA_HEADS

    def rowf(b, c):
        return b * nblk + c

    def rowb(b, c):
        return b * nblk + (nblk - 1 - c)

    def col(width, colstart, rowfn):
        return pl.BlockSpec((TQ, width), lambda b, c: (rowfn(b, c), colstart // width))

    scratch = [pltpu.VMEM((GLA_HEADS, GLA_DK_HEAD, GLA_DV_HEAD), F32),
               pltpu.VMEM((TQ, GLA_DK), BF16), pltpu.VMEM((TQ, GLA_DK), BF16), pltpu.VMEM((TQ, GLA_DK), BF16),
               pltpu.VMEM((nunits, GLA_CHUNK, GLA_CHUNK), BF16),
               pltpu.VMEM((nunits, GLA_DK_HEAD, GLA_DV_HEAD), F32),
               pltpu.VMEM((nunits, GLA_DK_HEAD, GLA_DV_HEAD), BF16)]
    of = pl.pallas_call(
        _gla_fwd_body,
        grid=(B, nblk),
        in_specs=[col(1024, U_QK, rowf), col(1024, U_V, rowf), col(SMALL_W, 0, rowf),
                  _resident((LANES, GLA_DK)), _resident((1, GLA_DK))],
        out_specs=pl.BlockSpec((TQ, 1024), lambda b, c: (rowf(b, c), 0)),
        out_shape=jax.ShapeDtypeStruct((T, GLA_DV), BF16),
        scratch_shapes=scratch,
        compiler_params=_cparams(("arbitrary", "arbitrary"), 40),
        name="gla_fwd",
    )(u, u, sm, p["gup_f"], p["gb_f"])
    return pl.pallas_call(
        _gla_bwd_body,
        grid=(B, nblk),
        in_specs=[col(1024, U_QK, rowb), col(1024, U_V, rowb), col(SMALL_W, 0, rowb),
                  col(1024, U_G, rowb), pl.BlockSpec((TQ, 1024), lambda b, c: (rowb(b, c), 0)),
                  _resident((LANES, GLA_DK)), _resident((1, GLA_DK)), _resident((1, GLA_DV_HEAD))],
        out_specs=pl.BlockSpec((TQ, 1024), lambda b, c: (rowb(b, c), 0)),
        out_shape=jax.ShapeDtypeStruct((T, GLA_DV), BF16),
        scratch_shapes=scratch,
        compiler_params=_cparams(("arbitrary", "arbitrary"), 40),
        name="gla_bwd",
    )(u, u, sm, u, of, p["gup_b"], p["gb_b"], p["gla_nw"])


def _na_body(q_ref, kp_ref, kc_ref, kn_ref, vp_ref, vc_ref, vn_ref, tab_ref, o_ref, kbuf, vbuf, s_s, p_s, *, nblk):
    i = pl.program_id(1)
    TQ = NA_TQ
    rows_total = nblk * NA_ROWS
    npairs = NA_HEADS // 2
    for buf, refs in ((kbuf, (kp_ref, kc_ref, kn_ref)), (vbuf, (vp_ref, vc_ref, vn_ref))):
        for n, r in enumerate(refs):
            buf[n * TQ:(n + 1) * TQ, :] = r[...]

    lane = lax.broadcasted_iota(jnp.int32, (GRID_W, LANES), 1)
    lo = lane < NA_HEAD_DIM
    nkeys = NA_KH * GRID_W

    cols = [slice(hp * LANES, (hp + 1) * LANES) for hp in range(npairs)]

    def rows_body(t, carry):
        units = []
        for a in range(NA_RPT):
            rho = t * NA_RPT + a
            r = i * NA_ROWS + rho
            rs = jnp.clip(r - NA_KH // 2, 0, rows_total - NA_KH)
            keys = pl.ds(pl.multiple_of((rs - (i - 1) * NA_ROWS) * GRID_W, GRID_W), nkeys)
            qrows = pl.ds(pl.multiple_of(rho * GRID_W, GRID_W), GRID_W)
            units += [(a * npairs + hp, hp, r - rs, keys, qrows) for hp in range(npairs)]
        ms = []
        for n, hp, delta, keys, qrows in units:
            qp = q_ref[qrows, cols[hp]] * (NA_HEAD_DIM ** -0.5)
            zero = jnp.zeros_like(qp)
            lhs = jnp.concatenate([jnp.where(lo, qp, zero), jnp.where(lo, zero, qp)], axis=0)
            bias = jnp.concatenate(
                [tab_ref[2 * jp - delta + NA_KH - 1, hp] for jp in range(NA_KH // 2)], axis=1)
            s = _dot_nt(lhs, kbuf[keys, cols[hp]]) + bias
            s_s[n] = s
            ms.append(jnp.max(s, axis=-1, keepdims=True))
        ls = []
        for n, hp, delta, keys, qrows in units:
            pexp = jnp.exp(s_s[n] - ms[n])
            ls.append(jnp.sum(pexp, axis=-1, keepdims=True))
            p_s[n] = pexp.astype(BF16)
        for n, hp, delta, keys, qrows in units:
            o = _dot(p_s[n], vbuf[keys, cols[hp]]) / ls[n]
            o_ref[qrows, cols[hp]] = jnp.where(lo, o[0:GRID_W], o[GRID_W:2 * GRID_W]).astype(BF16)
        return carry

    lax.fori_loop(0, NA_ROWS // NA_RPT, rows_body, 0)


def _na(u, B, L, tab):
    T = B * L
    TQ = NA_TQ
    nblk = L // TQ
    npairs = NA_HEADS // 2

    def blk(colstart, shift):
        def im(b, i):
            return (b * nblk + jnp.clip(i + shift, 0, nblk - 1), colstart // 1024)
        return pl.BlockSpec((TQ, 1024), im)

    return pl.pallas_call(
        functools.partial(_na_body, nblk=nblk),
        grid=(B, nblk),
        in_specs=[blk(U_QC, 0), blk(U_KC, -1), blk(U_KC, 0), blk(U_KC, 1),
                  blk(U_VC, -1), blk(U_VC, 0), blk(U_VC, 1),
                  _resident(tab.shape)],
        out_specs=pl.BlockSpec((TQ, 1024), lambda b, i: (b * nblk + i, 0)),
        out_shape=jax.ShapeDtypeStruct((T, 1024), BF16),
        scratch_shapes=[pltpu.VMEM((3 * TQ, 1024), BF16), pltpu.VMEM((3 * TQ, 1024), BF16),
                        pltpu.VMEM((NA_RPT * npairs, 2 * GRID_W, NA_KH * GRID_W), F32),
                        pltpu.VMEM((NA_RPT * npairs, 2 * GRID_W, NA_KH * GRID_W), BF16)],
        compiler_params=_cparams(("parallel", "parallel"), 48),
        name="na",
    )(u, u, u, u, u, u, u, tab)


def _na_bias_table(rpb):
    c = jnp.arange(GRID_W)[:, None]
    kc = jnp.arange(GRID_W)[None, :]
    cs = jnp.clip(c - NA_KW // 2, 0, GRID_W - NA_KW)
    valid = (kc >= cs) & (kc < cs + NA_KW)
    off = jnp.clip(kc - c + NA_KW - 1, 0, 2 * NA_KW - 2)
    t = jnp.where(valid, rpb.astype(F32)[:, :, off], MASK_NEG)
    tp = jnp.concatenate([t[:, :-1], t[:, 1:]], axis=-1)
    n = 2 * NA_KH - 2
    tp = tp.reshape(NA_HEADS // 2, 2, n, GRID_W, LANES).transpose(2, 0, 1, 3, 4)
    return tp.reshape(n, NA_HEADS // 2, 2 * GRID_W, LANES)


def _merge_body(x_ref, ya_ref, yb_ref, yc_ref, g0_ref, g1_ref, g2_ref, wa_ref, wb_ref, wc_ref, wo_ref, o_ref):
    def sig(g_ref):
        return jax.nn.sigmoid(g_ref[...].astype(F32))

    m = sig(g0_ref) * _dot(ya_ref[...], wa_ref[...])
    m = m + sig(g1_ref) * _dot(yb_ref[...], wb_ref[...])
    m = m + sig(g2_ref) * _dot(yc_ref[...], wc_ref[...])
    o_ref[...] = x_ref[...] + _dot(m.astype(BF16), wo_ref[...])


def _merge(x, ya, yb, yc, u, wa, wb, wc, wo):
    T = x.shape[0]
    tok = pl.BlockSpec((MERGE_TM, D_MODEL), lambda i: (i, 0))

    def gate(k):
        return pl.BlockSpec((MERGE_TM, D_MODEL), lambda i: (i, U_G0 // 1024 + k))

    w = _resident((D_MODEL, D_MODEL))
    return pl.pallas_call(
        _merge_body,
        grid=(T // MERGE_TM,),
        in_specs=[tok, tok, tok, tok, gate(0), gate(1), gate(2), w, w, w, w],
        out_specs=tok,
        out_shape=jax.ShapeDtypeStruct((T, D_MODEL), F32),
        compiler_params=_cparams(("parallel",), 40),
        name="merge",
    )(x, ya, yb, yc, u, u, u, wa, wb, wc, wo)


def _pad_rows(w, rows):
    return jnp.pad(w, ((0, rows - w.shape[0]), (0, 0)))


def _layer_params(i, ln_ffn1_w, ffn1_w_in, ffn1_w_out, ln_mix_w, w_in, ssm_conv_w, ssm_conv_b, ssm_dt_bias,
                  ssm_a_log, ssm_d, ssm_norm_w, gla_gate_up, gla_gate_b, gla_norm_w, na_rpb,
                  w_branch_a, w_branch_b, w_branch_c, w_out, ln_ffn2_w, ffn2_w_in, ffn2_w_out):
    w = w_in[i]
    o = 0
    parts = {}
    for name, width in (("z", 1024), ("xs", 1024), ("bc", 512), ("dt", 32), ("q", 512), ("k", 512),
                        ("v", 1024), ("g", 1024), ("dn", 32), ("qc", 1024), ("kc", 1024), ("vc", 1024),
                        ("gates", 3072)):
        parts[name] = w[:, o:o + width]
        o += width
    w_small = jnp.pad(jnp.concatenate([parts["dt"], parts["dn"]], axis=1), ((0, 0), (0, SMALL_W - 64))).astype(BF16)
    w_perm = jnp.concatenate([parts[n] for n in ("z", "xs", "q", "k", "v", "g", "qc", "kc", "vc", "gates", "bc")],
                             axis=1).astype(BF16)

    def gup(d):
        full = jnp.zeros((LANES, GLA_DK), F32).at[32 + 16 * d:48 + 16 * d].set(gla_gate_up[i, d].astype(F32))
        return full.astype(BF16)

    def pad_lanes(v2):
        return jnp.broadcast_to(v2.astype(F32).reshape(2 * SSM_HEADS, 1), (2 * SSM_HEADS, LANES))

    return dict(
        ln1=ln_ffn1_w[i].reshape(1, -1), f1_in=ffn1_w_in[i].astype(BF16), f1_out=ffn1_w_out[i].astype(BF16),
        ln_mix=ln_mix_w[i].reshape(1, -1), w_perm=w_perm, w_small=w_small,
        cwx=_pad_rows(ssm_conv_w[i][:, :1024], SUBLANES), cbx=ssm_conv_b[i][:1024].reshape(1, -1),
        cwb=_pad_rows(ssm_conv_w[i][:, 1024:], SUBLANES), cbb=ssm_conv_b[i][1024:].reshape(1, -1),
        dtb=pad_lanes(ssm_dt_bias[i]), alog=pad_lanes(ssm_a_log[i]),
        dskip=jnp.repeat(ssm_d[i].astype(F32), SSM_HEAD_DIM).reshape(1, -1),
        ssm_nw=ssm_norm_w[i].reshape(1, -1),
        gup_f=gup(0), gup_b=gup(1), gb_f=gla_gate_b[i, 0].reshape(1, -1), gb_b=gla_gate_b[i, 1].reshape(1, -1),
        gla_nw=gla_norm_w[i].reshape(1, -1),
        na_tab=_na_bias_table(na_rpb[i]),
        wa=w_branch_a[i].astype(BF16), wb=w_branch_b[i].astype(BF16), wc=w_branch_c[i].astype(BF16),
        wo=w_out[i].astype(BF16),
        ln2=ln_ffn2_w[i].reshape(1, -1), f2_in=ffn2_w_in[i].astype(BF16), f2_out=ffn2_w_out[i].astype(BF16),
    )


def _trunk(x3, layers, ln_final):
    B, L, D = x3.shape
    x = x3.reshape(B * L, D)
    for li, p in enumerate(layers):
        x, h, sm = _ffn(x, p["ln1"], p["f1_in"], p["f1_out"], mix=(p["ln_mix"], p["w_small"]))
        u = _inproj(h, p["w_perm"])
        ya = _ssd(u, sm, B, L, p)
        yb = _gla(u, sm, B, L, p)
        yc = _na(u, B, L, p["na_tab"])
        x = _merge(x, ya, yb, yc, u, p["wa"], p["wb"], p["wc"], p["wo"])
        x = _ffn(x, p["ln2"], p["f2_in"], p["f2_out"], lnf=ln_final if li == len(layers) - 1 else None)
    return x.reshape(B, L, D)


def kernel(x_prompt, x_sample, ln_ffn1_w, ffn1_w_in, ffn1_w_out, ln_mix_w, w_in, ssm_conv_w, ssm_conv_b, ssm_dt_bias, ssm_a_log, ssm_d, ssm_norm_w, gla_gate_up, gla_gate_b, gla_norm_w, na_rpb, w_branch_a, w_branch_b, w_branch_c, w_out, ln_ffn2_w, ffn2_w_in, ffn2_w_out, ln_final_w):
    layers = [_layer_params(i, ln_ffn1_w, ffn1_w_in, ffn1_w_out, ln_mix_w, w_in, ssm_conv_w, ssm_conv_b,
                            ssm_dt_bias, ssm_a_log, ssm_d, ssm_norm_w, gla_gate_up, gla_gate_b, gla_norm_w,
                            na_rpb, w_branch_a, w_branch_b, w_branch_c, w_out, ln_ffn2_w, ffn2_w_in, ffn2_w_out)
              for i in range(DEPTH)]
    ln_final = ln_final_w.reshape(1, -1)
    return (_trunk(x_prompt, layers, ln_final), _trunk(x_sample, layers, ln_final))
```

```python
import functools

import jax
import jax.numpy as jnp
from jax import lax
from jax.experimental import pallas as pl
from jax.experimental.pallas import tpu as pltpu

F32 = jnp.float32
BF16 = jnp.bfloat16

D_MODEL = 1024
DEPTH = 2
GRID_W = 64
SSM_HEADS = 16
SSM_HEAD_DIM = 64
SSM_WIDTH = 1024
SSM_GROUPS = 2
SSM_STATE = 128
SSM_CONV = 5
SSM_CHUNK = 128
GLA_HEADS = 4
GLA_DK_HEAD = 128
GLA_DV_HEAD = 256
GLA_DK = 512
GLA_DV = 1024
GLA_RANK = 16
GLA_NORMALIZER = 16.0
GLA_CHUNK = 64
NA_HEADS = 16
NA_HEAD_DIM = 64
NA_KH = 8
NA_KW = 16
D_FF = 2816
EPS = 1e-6

LANES = 128
SUBLANES = 8
BF16_ROWS = 16
MASK_NEG = -1e30
LOG2E = 1.4426950408889634

U_Z, U_XS, U_QK, U_V, U_G, U_QC, U_KC, U_VC, U_G0 = (i * 1024 for i in range(9))
U_BC = 11 * 1024
U_WIDTH = U_BC + 512
U_TN = U_WIDTH // 2
SMALL_W = LANES

MXU_N = 256
FFN_TM = 1024
FFN_CHUNKS = tuple((c, min(c + 3 * MXU_N, D_FF)) for c in range(0, D_FF, 3 * MXU_N))
PROJ_TM = 512
SSD_TQ = 512
GLA_TQ = 512
NA_ROWS = 8
NA_TQ = NA_ROWS * GRID_W
NA_RPT = 4
MERGE_TM = 512


def _cparams(sem, vmem_mb):
    return pltpu.CompilerParams(dimension_semantics=sem, vmem_limit_bytes=vmem_mb * 1024 * 1024)


def _resident(shape):
    nd = len(shape)
    return pl.BlockSpec(shape, lambda *_: (0,) * nd, pipeline_mode=pl.Buffered(1))


def _rms(x, w):
    return x * lax.rsqrt(jnp.mean(x * x, axis=-1, keepdims=True) + EPS) * w


def _silu(x):
    return x * jax.nn.sigmoid(x)


def _dot(a, b):
    return jnp.dot(a, b, preferred_element_type=F32)


def _dot_nt(a, b):
    return lax.dot_general(a, b, (((1,), (1,)), ((), ())), preferred_element_type=F32)


def _dot_tn(a, b):
    return lax.dot_general(a, b, (((0,), (0,)), ((), ())), preferred_element_type=F32)


def _split3(x):
    hi = x.astype(BF16)
    r1 = x - hi.astype(F32)
    mid = r1.astype(BF16)
    lo = (r1 - mid.astype(F32)).astype(BF16)
    return hi, mid, lo


def _lane_cumsum(x, reverse):
    n = x.shape[-1]
    lane = lax.broadcasted_iota(jnp.int32, x.shape, x.ndim - 1)
    s = 1
    while s < n:
        if reverse:
            x = x + jnp.where(lane < n - s, pltpu.roll(x, n - s, x.ndim - 1), 0.0)
        else:
            x = x + jnp.where(lane >= s, pltpu.roll(x, s, x.ndim - 1), 0.0)
        s *= 2
    return x


def _cumsum_dot(tri_bf, x):
    hi, mid, lo = _split3(x)
    return _dot(tri_bf, hi) + _dot(tri_bf, mid) + _dot(tri_bf, lo)


def _ffn_body(x_ref, lnw_ref, win_ref, wout_ref, *rest, final_norm, emit_mix):
    rest = list(rest)
    lnf_ref = rest.pop(0) if final_norm else None
    if emit_mix:
        lnm_ref, wsm_ref = rest.pop(0), rest.pop(0)
    o_ref = rest.pop(0)
    x = x_ref[...]
    h = _rms(x, lnw_ref[...]).astype(BF16)
    acc = None
    for c0, c1 in FFN_CHUNKS:
        a = _dot(h, win_ref[:, c0:c1])
        b = _dot(h, win_ref[:, D_FF + c0:D_FF + c1])
        p = _dot((_silu(a) * b).astype(BF16), wout_ref[c0:c1, :])
        acc = p if acc is None else acc + p
    y = x + 0.5 * acc
    if final_norm:
        y = _rms(y, lnf_ref[...])
    o_ref[...] = y
    if emit_mix:
        hm_ref, sm_ref = rest
        hm = _rms(y, lnm_ref[...]).astype(BF16)
        hm_ref[...] = hm
        sm_ref[...] = _dot(hm, wsm_ref[...])


def _ffn(x, lnw, w_in, w_out, lnf=None, mix=None):
    T = x.shape[0]
    tok = pl.BlockSpec((FFN_TM, D_MODEL), lambda i: (i, 0))
    in_specs = [tok, _resident((1, D_MODEL)), _resident((D_MODEL, 2 * D_FF)), _resident((D_FF, D_MODEL))]
    args = [x, lnw, w_in, w_out]
    out_specs = [tok]
    out_shape = [jax.ShapeDtypeStruct((T, D_MODEL), F32)]
    if lnf is not None:
        in_specs.append(_resident((1, D_MODEL)))
        args.append(lnf)
    if mix is not None:
        in_specs += [_resident((1, D_MODEL)), _resident((D_MODEL, SMALL_W))]
        args += list(mix)
        out_specs += [tok, pl.BlockSpec((FFN_TM, SMALL_W), lambda i: (i, 0))]
        out_shape += [jax.ShapeDtypeStruct((T, D_MODEL), BF16), jax.ShapeDtypeStruct((T, SMALL_W), F32)]
    out = pl.pallas_call(
        functools.partial(_ffn_body, final_norm=lnf is not None, emit_mix=mix is not None),
        grid=(T // FFN_TM,),
        in_specs=in_specs,
        out_specs=out_specs,
        out_shape=out_shape,
        compiler_params=_cparams(("parallel",), 56),
        name="ffn",
    )(*args)
    return out if mix is not None else out[0]


def _inproj_body(h_ref, w_ref, u_ref):
    u_ref[...] = _dot(h_ref[...], w_ref[...]).astype(BF16)


def _inproj(h, w):
    T = h.shape[0]
    return pl.pallas_call(
        _inproj_body,
        grid=(U_WIDTH // U_TN, T // PROJ_TM),
        in_specs=[pl.BlockSpec((PROJ_TM, D_MODEL), lambda j, i: (i, 0)),
                  pl.BlockSpec((D_MODEL, U_TN), lambda j, i: (0, j))],
        out_specs=pl.BlockSpec((PROJ_TM, U_TN), lambda j, i: (i, j)),
        out_shape=jax.ShapeDtypeStruct((T, U_WIDTH), BF16),
        compiler_params=_cparams(("arbitrary", "arbitrary"), 48),
        name="inproj",
    )(h, w)


def _ssd_decay_terms(sm_ref, dtb, alog, nch, reverse):
    Q = SSM_CHUNK
    off = SSM_HEADS if reverse else 0
    last = 0 if reverse else Q - 1
    raw = jnp.concatenate([sm_ref[c * Q:(c + 1) * Q, :].T[off:off + SSM_HEADS, :] for c in range(nch)], axis=0)
    tile = lambda p: jnp.concatenate([p[off:off + SSM_HEADS, :]] * nch, axis=0)
    dtT = jax.nn.softplus(raw + tile(dtb))
    kk = lax.broadcasted_iota(jnp.int32, (Q, Q), 0)
    jj = lax.broadcasted_iota(jnp.int32, (Q, Q), 1)
    triT = ((kk >= jj) if reverse else (kk <= jj)).astype(BF16)
    hi, mid, lo = _split3(dtT * (-jnp.exp(tile(alog))))
    csT = _dot(hi, triT) + _dot(mid, triT) + _dot(lo, triT)
    r2T = csT * LOG2E - jnp.log2(dtT)
    tot = csT[:, last:last + 1]
    WT = jnp.exp(tot - csT) * dtT
    elT = jnp.broadcast_to(jnp.exp(tot), csT.shape)
    cs = jnp.concatenate([csT, jnp.zeros((Q - nch * SSM_HEADS, Q), F32)], axis=0).T
    return r2T, WT, elT, cs * LOG2E, jnp.exp(cs)


def _ssd_chunk(x_get, Bc, Cc, terms, c, S_ref, reverse):
    Q = SSM_CHUNK
    r2T, WT, elT, c2, ecs = terms
    ii = lax.broadcasted_iota(jnp.int32, (Q, Q), 0)
    jj = lax.broadcasted_iota(jnp.int32, (Q, Q), 1)
    tri = (ii <= jj) if reverse else (ii >= jj)
    lo = jj < SSM_HEAD_DIM
    ys = []
    for g in range(SSM_GROUPS):
        Bg = Bc[:, g * SSM_STATE:(g + 1) * SSM_STATE]
        Cg = Cc[:, g * SSM_STATE:(g + 1) * SSM_STATE]
        cb = _dot_nt(Cg.astype(BF16), Bg.astype(BF16))
        BgT = Bg.T
        for pp in range(SSM_HEADS // SSM_GROUPS // 2):
            hp = g * (SSM_HEADS // SSM_GROUPS // 2) + pp
            xb = x_get(hp)
            Sp = S_ref[hp]
            rhs = jnp.concatenate([xb, Sp.astype(BF16)], axis=0)
            outs, dSs, scs = [], [], []
            for e in range(2):
                h = c * SSM_HEADS + 2 * hp + e
                Lm = jnp.where(tri, jnp.exp2(c2[:, h:h + 1] - r2T[h:h + 1, :]), 0.0)
                M = (cb * Lm).astype(BF16)
                Ce = (Cg * ecs[:, h:h + 1]).astype(BF16)
                outs.append(_dot(jnp.concatenate([M, Ce], axis=1), rhs))
                dSs.append(_dot((BgT * WT[h:h + 1, :]).astype(BF16), xb))
                scs.append(elT[h:h + 1, :])
            ys.append(jnp.where(lo, outs[0], outs[1]))
            S_ref[hp] = Sp * jnp.where(lo, scs[0], scs[1]) + jnp.where(lo, dSs[0], dSs[1])
    return ys


CONV_WIN = SSM_CHUNK + 2 * BF16_ROWS
CONV_SHIFTS = tuple(k for k in range(SSM_CONV) if k != SSM_CONV // 2)


def _conv_shift_matrix():
    t = lax.broadcasted_iota(jnp.int32, (len(CONV_SHIFTS) * SSM_CHUNK, CONV_WIN), 0)
    j = lax.broadcasted_iota(jnp.int32, (len(CONV_SHIFTS) * SSM_CHUNK, CONV_WIN), 1)
    hit = None
    for n, k in enumerate(CONV_SHIFTS):
        m = (t >= n * SSM_CHUNK) & (t < (n + 1) * SSM_CHUNK) & (
            j == t - n * SSM_CHUNK + BF16_ROWS - SSM_CONV // 2 + k)
        hit = m if hit is None else hit | m
    return jnp.where(hit, 1.0, 0.0).astype(BF16)


def _conv_silu_rows(ext_ref, shift, w_ref, b_ref, r0, c0, cw):
    win = ext_ref[r0:r0 + CONV_WIN, c0:c0 + cw]
    mid = SSM_CONV // 2
    acc = b_ref[:, c0:c0 + cw] + win[BF16_ROWS:BF16_ROWS + SSM_CHUNK].astype(F32) * w_ref[mid:mid + 1, c0:c0 + cw]
    shifted = _dot(shift, win)
    for n, k in enumerate(CONV_SHIFTS):
        acc = acc + shifted[n * SSM_CHUNK:(n + 1) * SSM_CHUNK] * w_ref[k:k + 1, c0:c0 + cw]
    return _silu(acc)


def _ssd_fwd_body(xs_ref, xsp_ref, xsn_ref, bc_ref, bcp_ref, bcn_ref, sm_ref,
                  cwx_ref, cbx_ref, cwb_ref, cbb_ref, dtb_ref, alog_ref,
                  yf_ref, xa_ref, ba_ref, extx, extb, S_ref, *, nblk):
    c = pl.program_id(1)
    TQ = SSD_TQ
    H = BF16_ROWS

    @pl.when(c == 0)
    def _():
        S_ref[...] = jnp.zeros_like(S_ref)

    first = c == 0
    final = c == nblk - 1
    for ext, cur, prv, nxt in ((extx, xs_ref, xsp_ref, xsn_ref), (extb, bc_ref, bcp_ref, bcn_ref)):
        ext[0:H, :] = jnp.where(first, jnp.zeros_like(prv[...]), prv[...])
        ext[H:H + TQ, :] = cur[...]
        ext[H + TQ:2 * H + TQ, :] = jnp.where(final, jnp.zeros_like(nxt[...]), nxt[...])

    shift = _conv_shift_matrix()
    terms = _ssd_decay_terms(sm_ref, dtb_ref[...], alog_ref[...], TQ // SSM_CHUNK, reverse=False)
    for q in range(TQ // SSM_CHUNK):
        r0 = q * SSM_CHUNK
        rows = slice(r0, r0 + SSM_CHUNK)
        for c0 in range(0, SSM_WIDTH, MXU_N):
            xa_ref[rows, c0:c0 + MXU_N] = _conv_silu_rows(extx, shift, cwx_ref, cbx_ref, r0, c0, MXU_N).astype(BF16)
        for c0 in range(0, 512, MXU_N):
            ba_ref[rows, c0:c0 + MXU_N] = _conv_silu_rows(extb, shift, cwb_ref, cbb_ref, r0, c0, MXU_N).astype(BF16)
        ys = _ssd_chunk(lambda hp: xa_ref[rows, hp * LANES:(hp + 1) * LANES],
                        ba_ref[rows, 0:256].astype(F32), ba_ref[rows, 256:512].astype(F32),
                        terms, q, S_ref, reverse=False)
        for hp, y in enumerate(ys):
            yf_ref[rows, hp * LANES:(hp + 1) * LANES] = y.astype(BF16)


def _ssd_bwd_body(xa_ref, ba_ref, sm_ref, z_ref, yf_ref, dtb_ref, alog_ref, dsk_ref, nw_ref,
                  ya_ref, S_ref):
    TQ = SSD_TQ

    @pl.when(pl.program_id(1) == 0)
    def _():
        S_ref[...] = jnp.zeros_like(S_ref)

    half = SSM_WIDTH // SSM_GROUPS
    terms = _ssd_decay_terms(sm_ref, dtb_ref[...], alog_ref[...], TQ // SSM_CHUNK, reverse=True)
    for q in reversed(range(TQ // SSM_CHUNK)):
        rows = slice(q * SSM_CHUNK, (q + 1) * SSM_CHUNK)
        ys = _ssd_chunk(lambda hp: xa_ref[rows, hp * LANES:(hp + 1) * LANES],
                        ba_ref[rows, 0:256].astype(F32), ba_ref[rows, 256:512].astype(F32),
                        terms, q, S_ref, reverse=True)
        for g in range(SSM_GROUPS):
            ts = []
            ssq = None
            for pp in range(4):
                hp = g * 4 + pp
                cols = slice(hp * LANES, (hp + 1) * LANES)
                t = ys[hp] + yf_ref[rows, cols].astype(F32) + dsk_ref[:, cols] * xa_ref[rows, cols].astype(F32)
                t = t * _silu(z_ref[rows, cols].astype(F32))
                ts.append(t)
                s = jnp.sum(t * t, axis=-1, keepdims=True)
                ssq = s if ssq is None else ssq + s
            inv = lax.rsqrt(ssq * (1.0 / half) + EPS)
            for pp in range(4):
                cols = slice((g * 4 + pp) * LANES, (g * 4 + pp + 1) * LANES)
                ya_ref[rows, cols] = (ts[pp] * inv * nw_ref[:, cols]).astype(BF16)


def _ssd(u, sm, B, L, p):
    T = B * L
    TQ = SSD_TQ
    nblk = L // TQ
    hb = TQ // BF16_ROWS
    nh = T // BF16_ROWS

    def rowf(b, c):
        return b * nblk + c

    def rowb(b, c):
        return b * nblk + (nblk - 1 - c)

    def col(width, colstart, rowfn):
        return pl.BlockSpec((TQ, width), lambda b, c: (rowfn(b, c), colstart // width))

    def halo(width, colstart, shift):
        def im(b, c):
            r = (b * nblk + c + shift) * hb - (1 - shift)
            return (jnp.clip(r, 0, nh - 1), colstart // width)
        return pl.BlockSpec((BF16_ROWS, width), im)

    state = pltpu.VMEM((SSM_HEADS // 2, SSM_STATE, LANES), F32)
    yf, xa, ba = pl.pallas_call(
        functools.partial(_ssd_fwd_body, nblk=nblk),
        grid=(B, nblk),
        in_specs=[col(1024, U_XS, rowf), halo(1024, U_XS, 0), halo(1024, U_XS, 1),
                  col(512, U_BC, rowf), halo(512, U_BC, 0), halo(512, U_BC, 1),
                  col(SMALL_W, 0, rowf),
                  _resident((SUBLANES, 1024)), _resident((1, 1024)),
                  _resident((SUBLANES, 512)), _resident((1, 512)),
                  _resident((2 * SSM_HEADS, LANES)), _resident((2 * SSM_HEADS, LANES))],
        out_specs=[pl.BlockSpec((TQ, 1024), lambda b, c: (rowf(b, c), 0)),
                   pl.BlockSpec((TQ, 1024), lambda b, c: (rowf(b, c), 0)),
                   pl.BlockSpec((TQ, 512), lambda b, c: (rowf(b, c), 0))],
        out_shape=[jax.ShapeDtypeStruct((T, 1024), BF16), jax.ShapeDtypeStruct((T, 1024), BF16),
                   jax.ShapeDtypeStruct((T, 512), BF16)],
        scratch_shapes=[pltpu.VMEM((TQ + 2 * BF16_ROWS, 1024), BF16),
                        pltpu.VMEM((TQ + 2 * BF16_ROWS, 512), BF16), state],
        compiler_params=_cparams(("arbitrary", "arbitrary"), 40),
        name="ssd_fwd",
    )(u, u, u, u, u, u, sm, p["cwx"], p["cbx"], p["cwb"], p["cbb"], p["dtb"], p["alog"])

    def rb(width):
        return pl.BlockSpec((TQ, width), lambda b, c: (rowb(b, c), 0))

    return pl.pallas_call(
        _ssd_bwd_body,
        grid=(B, nblk),
        in_specs=[rb(1024), rb(512), col(SMALL_W, 0, rowb), col(1024, U_Z, rowb), rb(1024),
                  _resident((2 * SSM_HEADS, LANES)), _resident((2 * SSM_HEADS, LANES)),
                  _resident((1, 1024)), _resident((1, 1024))],
        out_specs=rb(1024),
        out_shape=jax.ShapeDtypeStruct((T, 1024), BF16),
        scratch_shapes=[state],
        compiler_params=_cparams(("arbitrary", "arbitrary"), 40),
        name="ssd_bwd",
    )(xa, ba, sm, u, yf, p["dtb"], p["alog"], p["dskip"], p["ssm_nw"])


def _gla_block(qk_ref, v_ref, sm_ref, gup_ref, gb_ref, S_ref, qe_s, ke_s, kd_s, att_s, dS_s, Sb_s, reverse, emit):
    Q = GLA_CHUNK
    nch = GLA_TQ // Q
    last = 0 if reverse else Q - 1
    ii = lax.broadcasted_iota(jnp.int32, (Q, Q), 0)
    jj = lax.broadcasted_iota(jnp.int32, (Q, Q), 1)
    tri = (ii <= jj) if reverse else (ii >= jj)
    tri_bf = tri.astype(BF16)
    rows = [slice(c * Q, (c + 1) * Q) for c in range(nch)]
    heads = [slice(h * GLA_DK_HEAD, (h + 1) * GLA_DK_HEAD) for h in range(GLA_HEADS)]
    vcols = [slice(h * GLA_DV_HEAD, (h + 1) * GLA_DV_HEAD) for h in range(GLA_HEADS)]

    logits = [_dot(sm_ref[r, :].astype(BF16), gup_ref[...]) for r in rows]
    parts = [_split3(jax.nn.log_sigmoid(lg + gb_ref[...]) / GLA_NORMALIZER) for lg in logits]
    bcums = [_dot(tri_bf, hi) + _dot(tri_bf, mid) + _dot(tri_bf, lo) for hi, mid, lo in parts]
    bls = []
    for r, bcum in zip(rows, bcums):
        bl = bcum[last:last + 1, :]
        q = qk_ref[r, 0:GLA_DK].astype(F32)
        k = qk_ref[r, GLA_DK:2 * GLA_DK].astype(F32)
        qe_s[r, :] = ((q * (GLA_DK_HEAD ** -0.5)) * jnp.exp(bcum)).astype(BF16)
        ke_s[r, :] = (k * jnp.exp(-bcum)).astype(BF16)
        kd_s[r, :] = (k * jnp.exp(bl - bcum)).astype(BF16)
        bls.append(bl)

    for c, r in enumerate(rows):
        for h in range(GLA_HEADS):
            att = _dot_nt(qe_s[r, heads[h]], ke_s[r, heads[h]])
            att_s[c * GLA_HEADS + h] = jnp.where(tri, att, 0.0).astype(BF16)
    for c, r in enumerate(rows):
        for h in range(GLA_HEADS):
            dS_s[c * GLA_HEADS + h] = _dot_tn(kd_s[r, heads[h]], v_ref[r, vcols[h]])

    order = list(reversed(range(nch))) if reverse else list(range(nch))
    for h in range(GLA_HEADS):
        S = S_ref[h]
        for c in order:
            Sb_s[c * GLA_HEADS + h] = S.astype(BF16)
            d = jnp.exp(bls[c][:, heads[h]])
            dcol = jnp.broadcast_to(d, (GLA_DK_HEAD, GLA_DK_HEAD)).T
            S = S * jnp.concatenate([dcol, dcol], axis=1) + dS_s[c * GLA_HEADS + h]
        S_ref[h] = S

    for c, r in enumerate(rows):
        for h in range(GLA_HEADS):
            o = _dot(att_s[c * GLA_HEADS + h], v_ref[r, vcols[h]]) + _dot(qe_s[r, heads[h]], Sb_s[c * GLA_HEADS + h])
            emit(r, vcols[h], o)


def _gla_fwd_body(qk_ref, v_ref, sm_ref, gup_ref, gb_ref, of_ref, S_ref, *scratch):
    @pl.when(pl.program_id(1) == 0)
    def _():
        S_ref[...] = jnp.zeros_like(S_ref)

    def emit(r, cols, o):
        of_ref[r, cols] = o.astype(BF16)

    _gla_block(qk_ref, v_ref, sm_ref, gup_ref, gb_ref, S_ref, *scratch, reverse=False, emit=emit)


def _gla_bwd_body(qk_ref, v_ref, sm_ref, g_ref, of_ref, gup_ref, gb_ref, nw_ref, yb_ref, S_ref, *scratch):
    @pl.when(pl.program_id(1) == 0)
    def _():
        S_ref[...] = jnp.zeros_like(S_ref)

    def emit(r, cols, ob):
        o = ob + of_ref[r, cols].astype(F32)
        o = o * lax.rsqrt(jnp.mean(o * o, axis=-1, keepdims=True) + EPS) * nw_ref[...]
        yb_ref[r, cols] = (o * _silu(g_ref[r, cols].astype(F32))).astype(BF16)

    _gla_block(qk_ref, v_ref, sm_ref, gup_ref, gb_ref, S_ref, *scratch, reverse=True, emit=emit)


def _gla(u, sm, B, L, p):
    T = B * L
    TQ = GLA_TQ
    nblk = L // TQ
    nunits = (TQ // GLA_CHUNK) * GLA_HEADS

    def rowf(b, c):
        return b * nblk + c

    def rowb(b, c):
        return b * nblk + (nblk - 1 - c)

    def col(width, colstart, rowfn):
        return pl.BlockSpec((TQ, width), lambda b, c: (rowfn(b, c), colstart // width))

    scratch = [pltpu.VMEM((GLA_HEADS, GLA_DK_HEAD, GLA_DV_HEAD), F32),
               pltpu.VMEM((TQ, GLA_DK), BF16), pltpu.VMEM((TQ, GLA_DK), BF16), pltpu.VMEM((TQ, GLA_DK), BF16),
               pltpu.VMEM((nunits, GLA_CHUNK, GLA_CHUNK), BF16),
               pltpu.VMEM((nunits, GLA_DK_HEAD, GLA_DV_HEAD), F32),
               pltpu.VMEM((nunits, GLA_DK_HEAD, GLA_DV_HEAD), BF16)]
    of = pl.pallas_call(
        _gla_fwd_body,
        grid=(B, nblk),
        in_specs=[col(1024, U_QK, rowf), col(1024, U_V, rowf), col(SMALL_W, 0, rowf),
                  _resident((LANES, GLA_DK)), _resident((1, GLA_DK))],
        out_specs=pl.BlockSpec((TQ, 1024), lambda b, c: (rowf(b, c), 0)),
        out_shape=jax.ShapeDtypeStruct((T, GLA_DV), BF16),
        scratch_shapes=scratch,
        compiler_params=_cparams(("arbitrary", "arbitrary"), 40),
        name="gla_fwd",
    )(u, u, sm, p["gup_f"], p["gb_f"])
    return pl.pallas_call(
        _gla_bwd_body,
        grid=(B, nblk),
        in_specs=[col(1024, U_QK, rowb), col(1024, U_V, rowb), col(SMALL_W, 0, rowb),
                  col(1024, U_G, rowb), pl.BlockSpec((TQ, 1024), lambda b, c: (rowb(b, c), 0)),
                  _resident((LANES, GLA_DK)), _resident((1, GLA_DK)), _resident((1, GLA_DV_HEAD))],
        out_specs=pl.BlockSpec((TQ, 1024), lambda b, c: (rowb(b, c), 0)),
        out_shape=jax.ShapeDtypeStruct((T, GLA_DV), BF16),
        scratch_shapes=scratch,
        compiler_params=_cparams(("arbitrary", "arbitrary"), 40),
        name="gla_bwd",
    )(u, u, sm, u, of, p["gup_b"], p["gb_b"], p["gla_nw"])


def _na_body(q_ref, kp_ref, kc_ref, kn_ref, vp_ref, vc_ref, vn_ref, tab_ref, o_ref, kbuf, vbuf, s_s, p_s, *, nblk):
    i = pl.program_id(1)
    TQ = NA_TQ
    rows_total = nblk * NA_ROWS
    npairs = NA_HEADS // 2
    for buf, refs in ((kbuf, (kp_ref, kc_ref, kn_ref)), (vbuf, (vp_ref, vc_ref, vn_ref))):
        for n, r in enumerate(refs):
            buf[n * TQ:(n + 1) * TQ, :] = r[...]

    lane = lax.broadcasted_iota(jnp.int32, (GRID_W, LANES), 1)
    lo = lane < NA_HEAD_DIM
    nkeys = NA_KH * GRID_W

    cols = [slice(hp * LANES, (hp + 1) * LANES) for hp in range(npairs)]

    def rows_body(t, carry):
        units = []
        for a in range(NA_RPT):
            rho = t * NA_RPT + a
            r = i * NA_ROWS + rho
            rs = jnp.clip(r - NA_KH // 2, 0, rows_total - NA_KH)
            keys = pl.ds(pl.multiple_of((rs - (i - 1) * NA_ROWS) * GRID_W, GRID_W), nkeys)
            qrows = pl.ds(pl.multiple_of(rho * GRID_W, GRID_W), GRID_W)
            units += [(a * npairs + hp, hp, r - rs, keys, qrows) for hp in range(npairs)]
        ms = []
        for n, hp, delta, keys, qrows in units:
            qp = q_ref[qrows, cols[hp]] * (NA_HEAD_DIM ** -0.5)
            zero = jnp.zeros_like(qp)
            lhs = jnp.concatenate([jnp.where(lo, qp, zero), jnp.where(lo, zero, qp)], axis=0)
            bias = jnp.concatenate(
                [tab_ref[2 * jp - delta + NA_KH - 1, hp] for jp in range(NA_KH // 2)], axis=1)
            s = _dot_nt(lhs, kbuf[keys, cols[hp]]) + bias
            s_s[n] = s
            ms.append(jnp.max(s, axis=-1, keepdims=True))
        ls = []
        for n, hp, delta, keys, qrows in units:
            pexp = jnp.exp(s_s[n] - ms[n])
            ls.append(jnp.sum(pexp, axis=-1, keepdims=True))
            p_s[n] = pexp.astype(BF16)
        for n, hp, delta, keys, qrows in units:
            o = _dot(p_s[n], vbuf[keys, cols[hp]]) / ls[n]
            o_ref[qrows, cols[hp]] = jnp.where(lo, o[0:GRID_W], o[GRID_W:2 * GRID_W]).astype(BF16)
        return carry

    lax.fori_loop(0, NA_ROWS // NA_RPT, rows_body, 0)


def _na(u, B, L, tab):
    T = B * L
    TQ = NA_TQ
    nblk = L // TQ
    npairs = NA_HEADS // 2

    def blk(colstart, shift):
        def im(b, i):
            return (b * nblk + jnp.clip(i + shift, 0, nblk - 1), colstart // 1024)
        return pl.BlockSpec((TQ, 1024), im)

    return pl.pallas_call(
        functools.partial(_na_body, nblk=nblk),
        grid=(B, nblk),
        in_specs=[blk(U_QC, 0), blk(U_KC, -1), blk(U_KC, 0), blk(U_KC, 1),
                  blk(U_VC, -1), blk(U_VC, 0), blk(U_VC, 1),
                  _resident(tab.shape)],
        out_specs=pl.BlockSpec((TQ, 1024), lambda b, i: (b * nblk + i, 0)),
        out_shape=jax.ShapeDtypeStruct((T, 1024), BF16),
        scratch_shapes=[pltpu.VMEM((3 * TQ, 1024), BF16), pltpu.VMEM((3 * TQ, 1024), BF16),
                        pltpu.VMEM((NA_RPT * npairs, 2 * GRID_W, NA_KH * GRID_W), F32),
                        pltpu.VMEM((NA_RPT * npairs, 2 * GRID_W, NA_KH * GRID_W), BF16)],
        compiler_params=_cparams(("parallel", "parallel"), 48),
        name="na",
    )(u, u, u, u, u, u, u, tab)


def _na_bias_table(rpb):
    c = jnp.arange(GRID_W)[:, None]
    kc = jnp.arange(GRID_W)[None, :]
    cs = jnp.clip(c - NA_KW // 2, 0, GRID_W - NA_KW)
    valid = (kc >= cs) & (kc < cs + NA_KW)
    off = jnp.clip(kc - c + NA_KW - 1, 0, 2 * NA_KW - 2)
    t = jnp.where(valid, rpb.astype(F32)[:, :, off], MASK_NEG)
    tp = jnp.concatenate([t[:, :-1], t[:, 1:]], axis=-1)
    n = 2 * NA_KH - 2
    tp = tp.reshape(NA_HEADS // 2, 2, n, GRID_W, LANES).transpose(2, 0, 1, 3, 4)
    return tp.reshape(n, NA_HEADS // 2, 2 * GRID_W, LANES)


def _merge_body(x_ref, ya_ref, yb_ref, yc_ref, g0_ref, g1_ref, g2_ref, wa_ref, wb_ref, wc_ref, wo_ref, o_ref):
    def sig(g_ref):
        return jax.nn.sigmoid(g_ref[...].astype(F32))

    m = sig(g0_ref) * _dot(ya_ref[...], wa_ref[...])
    m = m + sig(g1_ref) * _dot(yb_ref[...], wb_ref[...])
    m = m + sig(g2_ref) * _dot(yc_ref[...], wc_ref[...])
    o_ref[...] = x_ref[...] + _dot(m.astype(BF16), wo_ref[...])


def _merge(x, ya, yb, yc, u, wa, wb, wc, wo):
    T = x.shape[0]
    tok = pl.BlockSpec((MERGE_TM, D_MODEL), lambda i: (i, 0))

    def gate(k):
        return pl.BlockSpec((MERGE_TM, D_MODEL), lambda i: (i, U_G0 // 1024 + k))

    w = _resident((D_MODEL, D_MODEL))
    return pl.pallas_call(
        _merge_body,
        grid=(T // MERGE_TM,),
        in_specs=[tok, tok, tok, tok, gate(0), gate(1), gate(2), w, w, w, w],
        out_specs=tok,
        out_shape=jax.ShapeDtypeStruct((T, D_MODEL), F32),
        compiler_params=_cparams(("parallel",), 40),
        name="merge",
    )(x, ya, yb, yc, u, u, u, wa, wb, wc, wo)


def _pad_rows(w, rows):
    return jnp.pad(w, ((0, rows - w.shape[0]), (0, 0)))


def _layer_params(i, ln_ffn1_w, ffn1_w_in, ffn1_w_out, ln_mix_w, w_in, ssm_conv_w, ssm_conv_b, ssm_dt_bias,
                  ssm_a_log, ssm_d, ssm_norm_w, gla_gate_up, gla_gate_b, gla_norm_w, na_rpb,
                  w_branch_a, w_branch_b, w_branch_c, w_out, ln_ffn2_w, ffn2_w_in, ffn2_w_out):
    w = w_in[i]
    o = 0
    parts = {}
    for name, width in (("z", 1024), ("xs", 1024), ("bc", 512), ("dt", 32), ("q", 512), ("k", 512),
                        ("v", 1024), ("g", 1024), ("dn", 32), ("qc", 1024), ("kc", 1024), ("vc", 1024),
                        ("gates", 3072)):
        parts[name] = w[:, o:o + width]
        o += width
    w_small = jnp.pad(jnp.concatenate([parts["dt"], parts["dn"]], axis=1), ((0, 0), (0, SMALL_W - 64))).astype(BF16)
    w_perm = jnp.concatenate([parts[n] for n in ("z", "xs", "q", "k", "v", "g", "qc", "kc", "vc", "gates", "bc")],
                             axis=1).astype(BF16)

    def gup(d):
        full = jnp.zeros((LANES, GLA_DK), F32).at[32 + 16 * d:48 + 16 * d].set(gla_gate_up[i, d].astype(F32))
        return full.astype(BF16)

    def pad_lanes(v2):
        return jnp.broadcast_to(v2.astype(F32).reshape(2 * SSM_HEADS, 1), (2 * SSM_HEADS, LANES))

    return dict(
        ln1=ln_ffn1_w[i].reshape(1, -1), f1_in=ffn1_w_in[i].astype(BF16), f1_out=ffn1_w_out[i].astype(BF16),
        ln_mix=ln_mix_w[i].reshape(1, -1), w_perm=w_perm, w_small=w_small,
        cwx=_pad_rows(ssm_conv_w[i][:, :1024], SUBLANES), cbx=ssm_conv_b[i][:1024].reshape(1, -1),
        cwb=_pad_rows(ssm_conv_w[i][:, 1024:], SUBLANES), cbb=ssm_conv_b[i][1024:].reshape(1, -1),
        dtb=pad_lanes(ssm_dt_bias[i]), alog=pad_lanes(ssm_a_log[i]),
        dskip=jnp.repeat(ssm_d[i].astype(F32), SSM_HEAD_DIM).reshape(1, -1),
        ssm_nw=ssm_norm_w[i].reshape(1, -1),
        gup_f=gup(0), gup_b=gup(1), gb_f=gla_gate_b[i, 0].reshape(1, -1), gb_b=gla_gate_b[i, 1].reshape(1, -1),
        gla_nw=gla_norm_w[i].reshape(1, -1),
        na_tab=_na_bias_table(na_rpb[i]),
        wa=w_branch_a[i].astype(BF16), wb=w_branch_b[i].astype(BF16), wc=w_branch_c[i].astype(BF16),
        wo=w_out[i].astype(BF16),
        ln2=ln_ffn2_w[i].reshape(1, -1), f2_in=ffn2_w_in[i].astype(BF16), f2_out=ffn2_w_out[i].astype(BF16),
    )


def _trunk(x3, layers, ln_final):
    B, L, D = x3.shape
    x = x3.reshape(B * L, D)
    for li, p in enumerate(layers):
        x, h, sm = _ffn(x, p["ln1"], p["f1_in"], p["f1_out"], mix=(p["ln_mix"], p["w_small"]))
        u = _inproj(h, p["w_perm"])
        ya = _ssd(u, sm, B, L, p)
        yb = _gla(u, sm, B, L, p)
        yc = _na(u, B, L, p["na_tab"])
        x = _merge(x, ya, yb, yc, u, p["wa"], p["wb"], p["wc"], p["wo"])
        x = _ffn(x, p["ln2"], p["f2_in"], p["f2_out"], lnf=ln_final if li == len(layers) - 1 else None)
    return x.reshape(B, L, D)


def kernel(x_prompt, x_sample, ln_ffn1_w, ffn1_w_in, ffn1_w_out, ln_mix_w, w_in, ssm_conv_w, ssm_conv_b, ssm_dt_bias, ssm_a_log, ssm_d, ssm_norm_w, gla_gate_up, gla_gate_b, gla_norm_w, na_rpb, w_branch_a, w_branch_b, w_branch_c, w_out, ln_ffn2_w, ffn2_w_in, ffn2_w_out, ln_final_w):
    layers = [_layer_params(i, ln_ffn1_w, ffn1_w_in, ffn1_w_out, ln_mix_w, w_in, ssm_conv_w, ssm_conv_b,
                            ssm_dt_bias, ssm_a_log, ssm_d, ssm_norm_w, gla_gate_up, gla_gate_b, gla_norm_w,
                            na_rpb, w_branch_a, w_branch_b, w_branch_c, w_out, ln_ffn2_w, ffn2_w_in, ffn2_w_out)
              for i in range(DEPTH)]
    ln_final = ln_final_w.reshape(1, -1)
    return (_trunk(x_prompt, layers, ln_final), _trunk(x_sample, layers, ln_final))
```

```python
import functools

import jax
import jax.numpy as jnp
from jax import lax
from jax.experimental import pallas as pl
from jax.experimental.pallas import tpu as pltpu

F32 = jnp.float32
BF16 = jnp.bfloat16

D_MODEL = 1024
DEPTH = 2
GRID_W = 64
SSM_HEADS = 16
SSM_HEAD_DIM = 64
SSM_WIDTH = 1024
SSM_GROUPS = 2
SSM_STATE = 128
SSM_CONV = 5
SSM_CHUNK = 128
GLA_HEADS = 4
GLA_DK_HEAD = 128
GLA_DV_HEAD = 256
GLA_DK = 512
GLA_DV = 1024
GLA_RANK = 16
GLA_NORMALIZER = 16.0
GLA_CHUNK = 64
NA_HEADS = 16
NA_HEAD_DIM = 64
NA_KH = 8
NA_KW = 16
D_FF = 2816
EPS = 1e-6

LANES = 128
SUBLANES = 8
BF16_ROWS = 16
MASK_NEG = -1e30
LOG2E = 1.4426950408889634

U_Z, U_XS, U_QK, U_V, U_G, U_QC, U_KC, U_VC, U_G0 = (i * 1024 for i in range(9))
U_BC = 11 * 1024
U_WIDTH = U_BC + 512
U_TN = U_WIDTH // 2
SMALL_W = LANES

MXU_N = 256
FFN_TM = 1024
FFN_CHUNKS = tuple((c, min(c + 3 * MXU_N, D_FF)) for c in range(0, D_FF, 3 * MXU_N))
PROJ_TM = 512
SSD_TQ = 512
GLA_TQ = 512
NA_ROWS = 8
NA_TQ = NA_ROWS * GRID_W
NA_RPT = 4
MERGE_TM = 512


def _cparams(sem, vmem_mb):
    return pltpu.CompilerParams(dimension_semantics=sem, vmem_limit_bytes=vmem_mb * 1024 * 1024)


def _resident(shape):
    nd = len(shape)
    return pl.BlockSpec(shape, lambda *_: (0,) * nd, pipeline_mode=pl.Buffered(1))


def _rms(x, w):
    return x * lax.rsqrt(jnp.mean(x * x, axis=-1, keepdims=True) + EPS) * w


def _silu(x):
    return x * jax.nn.sigmoid(x)


def _dot(a, b):
    return jnp.dot(a, b, preferred_element_type=F32)


def _dot_nt(a, b):
    return lax.dot_general(a, b, (((1,), (1,)), ((), ())), preferred_element_type=F32)


def _dot_tn(a, b):
    return lax.dot_general(a, b, (((0,), (0,)), ((), ())), preferred_element_type=F32)


def _split3(x):
    hi = x.astype(BF16)
    r1 = x - hi.astype(F32)
    mid = r1.astype(BF16)
    lo = (r1 - mid.astype(F32)).astype(BF16)
    return hi, mid, lo


def _lane_cumsum(x, reverse):
    n = x.shape[-1]
    lane = lax.broadcasted_iota(jnp.int32, x.shape, x.ndim - 1)
    s = 1
    while s < n:
        if reverse:
            x = x + jnp.where(lane < n - s, pltpu.roll(x, n - s, x.ndim - 1), 0.0)
        else:
            x = x + jnp.where(lane >= s, pltpu.roll(x, s, x.ndim - 1), 0.0)
        s *= 2
    return x


def _cumsum_dot(tri_bf, x):
    hi, mid, lo = _split3(x)
    return _dot(tri_bf, hi) + _dot(tri_bf, mid) + _dot(tri_bf, lo)


def _ffn_body(x_ref, lnw_ref, win_ref, wout_ref, *rest, final_norm, emit_mix):
    rest = list(rest)
    lnf_ref = rest.pop(0) if final_norm else None
    if emit_mix:
        lnm_ref, wsm_ref = rest.pop(0), rest.pop(0)
    o_ref = rest.pop(0)
    x = x_ref[...]
    h = _rms(x, lnw_ref[...]).astype(BF16)
    acc = None
    for c0, c1 in FFN_CHUNKS:
        a = _dot(h, win_ref[:, c0:c1])
        b = _dot(h, win_ref[:, D_FF + c0:D_FF + c1])
        p = _dot((_silu(a) * b).astype(BF16), wout_ref[c0:c1, :])
        acc = p if acc is None else acc + p
    y = x + 0.5 * acc
    if final_norm:
        y = _rms(y, lnf_ref[...])
    o_ref[...] = y
    if emit_mix:
        hm_ref, sm_ref = rest
        hm = _rms(y, lnm_ref[...]).astype(BF16)
        hm_ref[...] = hm
        sm_ref[...] = _dot(hm, wsm_ref[...])


def _ffn(x, lnw, w_in, w_out, lnf=None, mix=None):
    T = x.shape[0]
    tok = pl.BlockSpec((FFN_TM, D_MODEL), lambda i: (i, 0))
    in_specs = [tok, _resident((1, D_MODEL)), _resident((D_MODEL, 2 * D_FF)), _resident((D_FF, D_MODEL))]
    args = [x, lnw, w_in, w_out]
    out_specs = [tok]
    out_shape = [jax.ShapeDtypeStruct((T, D_MODEL), F32)]
    if lnf is not None:
        in_specs.append(_resident((1, D_MODEL)))
        args.append(lnf)
    if mix is not None:
        in_specs += [_resident((1, D_MODEL)), _resident((D_MODEL, SMALL_W))]
        args += list(mix)
        out_specs += [tok, pl.BlockSpec((FFN_TM, SMALL_W), lambda i: (i, 0))]
        out_shape += [jax.ShapeDtypeStruct((T, D_MODEL), BF16), jax.ShapeDtypeStruct((T, SMALL_W), F32)]
    out = pl.pallas_call(
        functools.partial(_ffn_body, final_norm=lnf is not None, emit_mix=mix is not None),
        grid=(T // FFN_TM,),
        in_specs=in_specs,
        out_specs=out_specs,
        out_shape=out_shape,
        compiler_params=_cparams(("parallel",), 56),
        name="ffn",
    )(*args)
    return out if mix is not None else out[0]


def _inproj_body(h_ref, w_ref, u_ref):
    u_ref[...] = _dot(h_ref[...], w_ref[...]).astype(BF16)


def _inproj(h, w):
    T = h.shape[0]
    return pl.pallas_call(
        _inproj_body,
        grid=(U_WIDTH // U_TN, T // PROJ_TM),
        in_specs=[pl.BlockSpec((PROJ_TM, D_MODEL), lambda j, i: (i, 0)),
                  pl.BlockSpec((D_MODEL, U_TN), lambda j, i: (0, j))],
        out_specs=pl.BlockSpec((PROJ_TM, U_TN), lambda j, i: (i, j)),
        out_shape=jax.ShapeDtypeStruct((T, U_WIDTH), BF16),
        compiler_params=_cparams(("arbitrary", "arbitrary"), 48),
        name="inproj",
    )(h, w)


def _ssd_decay_terms(sm_ref, dtb, alog, nch, reverse):
    Q = SSM_CHUNK
    off = SSM_HEADS if reverse else 0
    last = 0 if reverse else Q - 1
    raw = jnp.concatenate([sm_ref[c * Q:(c + 1) * Q, :].T[off:off + SSM_HEADS, :] for c in range(nch)], axis=0)
    tile = lambda p: jnp.concatenate([p[off:off + SSM_HEADS, :]] * nch, axis=0)
    dtT = jax.nn.softplus(raw + tile(dtb))
    kk = lax.broadcasted_iota(jnp.int32, (Q, Q), 0)
    jj = lax.broadcasted_iota(jnp.int32, (Q, Q), 1)
    triT = ((kk >= jj) if reverse else (kk <= jj)).astype(BF16)
    hi, mid, lo = _split3(dtT * (-jnp.exp(tile(alog))))
    csT = _dot(hi, triT) + _dot(mid, triT) + _dot(lo, triT)
    r2T = csT * LOG2E - jnp.log2(dtT)
    tot = csT[:, last:last + 1]
    WT = jnp.exp(tot - csT) * dtT
    elT = jnp.broadcast_to(jnp.exp(tot), csT.shape)
    cs = jnp.concatenate([csT, jnp.zeros((Q - nch * SSM_HEADS, Q), F32)], axis=0).T
    return r2T, WT, elT, cs * LOG2E, jnp.exp(cs)


def _ssd_chunk(x_get, Bc, Cc, terms, c, S_ref, reverse):
    Q = SSM_CHUNK
    r2T, WT, elT, c2, ecs = terms
    mxu_bcast = reverse
    ii = lax.broadcasted_iota(jnp.int32, (Q, Q), 0)
    jj = lax.broadcasted_iota(jnp.int32, (Q, Q), 1)
    tri = (ii <= jj) if reverse else (ii >= jj)
    lo = jj < SSM_HEAD_DIM
    if mxu_bcast:
        kk = lax.broadcasted_iota(jnp.int32, (Q, SSM_HEADS * LANES), 0)
        nn = lax.broadcasted_iota(jnp.int32, (Q, SSM_HEADS * LANES), 1)
        sel = jnp.where(kk == c * SSM_HEADS + jnp.right_shift(nn, 7), 1.0, 0.0).astype(BF16)
        eb = _dot(ecs.astype(BF16), sel)
    ys = []
    for g in range(SSM_GROUPS):
        Bg = Bc[:, g * SSM_STATE:(g + 1) * SSM_STATE]
        Cg = Cc[:, g * SSM_STATE:(g + 1) * SSM_STATE]
        cb = _dot_nt(Cg.astype(BF16), Bg.astype(BF16))
        BgT = Bg.T
        for pp in range(SSM_HEADS // SSM_GROUPS // 2):
            hp = g * (SSM_HEADS // SSM_GROUPS // 2) + pp
            xb = x_get(hp)
            Sp = S_ref[hp]
            rhs = jnp.concatenate([xb, Sp.astype(BF16)], axis=0)
            outs, dSs, scs = [], [], []
            for e in range(2):
                h = c * SSM_HEADS + 2 * hp + e
                Lm = jnp.where(tri, jnp.exp2(c2[:, h:h + 1] - r2T[h:h + 1, :]), 0.0)
                M = (cb * Lm).astype(BF16)
                if mxu_bcast:
                    ecol = eb[:, (2 * hp + e) * LANES:(2 * hp + e + 1) * LANES]
                else:
                    ecol = ecs[:, h:h + 1]
                Ce = (Cg * ecol).astype(BF16)
                outs.append(_dot(jnp.concatenate([M, Ce], axis=1), rhs))
                dSs.append(_dot((BgT * WT[h:h + 1, :]).astype(BF16), xb))
                scs.append(elT[h:h + 1, :])
            ys.append(jnp.where(lo, outs[0], outs[1]))
            S_ref[hp] = Sp * jnp.where(lo, scs[0], scs[1]) + jnp.where(lo, dSs[0], dSs[1])
    return ys


CONV_WIN = SSM_CHUNK + 2 * BF16_ROWS
CONV_SHIFTS = tuple(k for k in range(SSM_CONV) if k != SSM_CONV // 2)


def _conv_shift_matrix():
    t = lax.broadcasted_iota(jnp.int32, (len(CONV_SHIFTS) * SSM_CHUNK, CONV_WIN), 0)
    j = lax.broadcasted_iota(jnp.int32, (len(CONV_SHIFTS) * SSM_CHUNK, CONV_WIN), 1)
    hit = None
    for n, k in enumerate(CONV_SHIFTS):
        m = (t >= n * SSM_CHUNK) & (t < (n + 1) * SSM_CHUNK) & (
            j == t - n * SSM_CHUNK + BF16_ROWS - SSM_CONV // 2 + k)
        hit = m if hit is None else hit | m
    return jnp.where(hit, 1.0, 0.0).astype(BF16)


def _conv_silu_rows(ext_ref, shift, w_ref, b_ref, r0, c0, cw):
    win = ext_ref[r0:r0 + CONV_WIN, c0:c0 + cw]
    mid = SSM_CONV // 2
    acc = b_ref[:, c0:c0 + cw] + win[BF16_ROWS:BF16_ROWS + SSM_CHUNK].astype(F32) * w_ref[mid:mid + 1, c0:c0 + cw]
    shifted = _dot(shift, win)
    for n, k in enumerate(CONV_SHIFTS):
        acc = acc + shifted[n * SSM_CHUNK:(n + 1) * SSM_CHUNK] * w_ref[k:k + 1, c0:c0 + cw]
    return _silu(acc)


def _ssd_fwd_body(xs_ref, xsp_ref, xsn_ref, bc_ref, bcp_ref, bcn_ref, sm_ref,
                  cwx_ref, cbx_ref, cwb_ref, cbb_ref, dtb_ref, alog_ref,
                  yf_ref, xa_ref, ba_ref, extx, extb, S_ref, *, nblk):
    c = pl.program_id(1)
    TQ = SSD_TQ
    H = BF16_ROWS

    @pl.when(c == 0)
    def _():
        S_ref[...] = jnp.zeros_like(S_ref)

    first = c == 0
    final = c == nblk - 1
    for ext, cur, prv, nxt in ((extx, xs_ref, xsp_ref, xsn_ref), (extb, bc_ref, bcp_ref, bcn_ref)):
        ext[0:H, :] = jnp.where(first, jnp.zeros_like(prv[...]), prv[...])
        ext[H:H + TQ, :] = cur[...]
        ext[H + TQ:2 * H + TQ, :] = jnp.where(final, jnp.zeros_like(nxt[...]), nxt[...])

    shift = _conv_shift_matrix()
    terms = _ssd_decay_terms(sm_ref, dtb_ref[...], alog_ref[...], TQ // SSM_CHUNK, reverse=False)
    for q in range(TQ // SSM_CHUNK):
        r0 = q * SSM_CHUNK
        rows = slice(r0, r0 + SSM_CHUNK)
        for c0 in range(0, SSM_WIDTH, MXU_N):
            xa_ref[rows, c0:c0 + MXU_N] = _conv_silu_rows(extx, shift, cwx_ref, cbx_ref, r0, c0, MXU_N).astype(BF16)
        for c0 in range(0, 512, MXU_N):
            ba_ref[rows, c0:c0 + MXU_N] = _conv_silu_rows(extb, shift, cwb_ref, cbb_ref, r0, c0, MXU_N).astype(BF16)
        ys = _ssd_chunk(lambda hp: xa_ref[rows, hp * LANES:(hp + 1) * LANES],
                        ba_ref[rows, 0:256].astype(F32), ba_ref[rows, 256:512].astype(F32),
                        terms, q, S_ref, reverse=False)
        for hp, y in enumerate(ys):
            yf_ref[rows, hp * LANES:(hp + 1) * LANES] = y.astype(BF16)


def _ssd_bwd_body(xa_ref, ba_ref, sm_ref, z_ref, yf_ref, dtb_ref, alog_ref, dsk_ref, nw_ref,
                  ya_ref, S_ref):
    TQ = SSD_TQ

    @pl.when(pl.program_id(1) == 0)
    def _():
        S_ref[...] = jnp.zeros_like(S_ref)

    half = SSM_WIDTH // SSM_GROUPS
    terms = _ssd_decay_terms(sm_ref, dtb_ref[...], alog_ref[...], TQ // SSM_CHUNK, reverse=True)
    for q in reversed(range(TQ // SSM_CHUNK)):
        rows = slice(q * SSM_CHUNK, (q + 1) * SSM_CHUNK)
        ys = _ssd_chunk(lambda hp: xa_ref[rows, hp * LANES:(hp + 1) * LANES],
                        ba_ref[rows, 0:256].astype(F32), ba_ref[rows, 256:512].astype(F32),
                        terms, q, S_ref, reverse=True)
        for g in range(SSM_GROUPS):
            ts = []
            ssq = None
            for pp in range(4):
                hp = g * 4 + pp
                cols = slice(hp * LANES, (hp + 1) * LANES)
                t = ys[hp] + yf_ref[rows, cols].astype(F32) + dsk_ref[:, cols] * xa_ref[rows, cols].astype(F32)
                t = t * _silu(z_ref[rows, cols].astype(F32))
                ts.append(t)
                s = jnp.sum(t * t, axis=-1, keepdims=True)
                ssq = s if ssq is None else ssq + s
            inv = lax.rsqrt(ssq * (1.0 / half) + EPS)
            for pp in range(4):
                cols = slice((g * 4 + pp) * LANES, (g * 4 + pp + 1) * LANES)
                ya_ref[rows, cols] = (ts[pp] * inv * nw_ref[:, cols]).astype(BF16)


def _ssd(u, sm, B, L, p):
    T = B * L
    TQ = SSD_TQ
    nblk = L // TQ
    hb = TQ // BF16_ROWS
    nh = T // BF16_ROWS

    def rowf(b, c):
        return b * nblk + c

    def rowb(b, c):
        return b * nblk + (nblk - 1 - c)

    def col(width, colstart, rowfn):
        return pl.BlockSpec((TQ, width), lambda b, c: (rowfn(b, c), colstart // width))

    def halo(width, colstart, shift):
        def im(b, c):
            r = (b * nblk + c + shift) * hb - (1 - shift)
            return (jnp.clip(r, 0, nh - 1), colstart // width)
        return pl.BlockSpec((BF16_ROWS, width), im)

    state = pltpu.VMEM((SSM_HEADS // 2, SSM_STATE, LANES), F32)
    yf, xa, ba = pl.pallas_call(
        functools.partial(_ssd_fwd_body, nblk=nblk),
        grid=(B, nblk),
        in_specs=[col(1024, U_XS, rowf), halo(1024, U_XS, 0), halo(1024, U_XS, 1),
                  col(512, U_BC, rowf), halo(512, U_BC, 0), halo(512, U_BC, 1),
                  col(SMALL_W, 0, rowf),
                  _resident((SUBLANES, 1024)), _resident((1, 1024)),
                  _resident((SUBLANES, 512)), _resident((1, 512)),
                  _resident((2 * SSM_HEADS, LANES)), _resident((2 * SSM_HEADS, LANES))],
        out_specs=[pl.BlockSpec((TQ, 1024), lambda b, c: (rowf(b, c), 0)),
                   pl.BlockSpec((TQ, 1024), lambda b, c: (rowf(b, c), 0)),
                   pl.BlockSpec((TQ, 512), lambda b, c: (rowf(b, c), 0))],
        out_shape=[jax.ShapeDtypeStruct((T, 1024), BF16), jax.ShapeDtypeStruct((T, 1024), BF16),
                   jax.ShapeDtypeStruct((T, 512), BF16)],
        scratch_shapes=[pltpu.VMEM((TQ + 2 * BF16_ROWS, 1024), BF16),
                        pltpu.VMEM((TQ + 2 * BF16_ROWS, 512), BF16), state],
        compiler_params=_cparams(("arbitrary", "arbitrary"), 40),
        name="ssd_fwd",
    )(u, u, u, u, u, u, sm, p["cwx"], p["cbx"], p["cwb"], p["cbb"], p["dtb"], p["alog"])

    def rb(width):
        return pl.BlockSpec((TQ, width), lambda b, c: (rowb(b, c), 0))

    return pl.pallas_call(
        _ssd_bwd_body,
        grid=(B, nblk),
        in_specs=[rb(1024), rb(512), col(SMALL_W, 0, rowb), col(1024, U_Z, rowb), rb(1024),
                  _resident((2 * SSM_HEADS, LANES)), _resident((2 * SSM_HEADS, LANES)),
                  _resident((1, 1024)), _resident((1, 1024))],
        out_specs=rb(1024),
        out_shape=jax.ShapeDtypeStruct((T, 1024), BF16),
        scratch_shapes=[state],
        compiler_params=_cparams(("arbitrary", "arbitrary"), 40),
        name="ssd_bwd",
    )(xa, ba, sm, u, yf, p["dtb"], p["alog"], p["dskip"], p["ssm_nw"])


def _gla_block(qk_ref, v_ref, sm_ref, gup_ref, gb_ref, S_ref, qe_s, ke_s, kd_s, att_s, dS_s, Sb_s, reverse, emit):
    Q = GLA_CHUNK
    nch = GLA_TQ // Q
    last = 0 if reverse else Q - 1
    ii = lax.broadcasted_iota(jnp.int32, (Q, Q), 0)
    jj = lax.broadcasted_iota(jnp.int32, (Q, Q), 1)
    tri = (ii <= jj) if reverse else (ii >= jj)
    tri_bf = tri.astype(BF16)
    rows = [slice(c * Q, (c + 1) * Q) for c in range(nch)]
    heads = [slice(h * GLA_DK_HEAD, (h + 1) * GLA_DK_HEAD) for h in range(GLA_HEADS)]
    vcols = [slice(h * GLA_DV_HEAD, (h + 1) * GLA_DV_HEAD) for h in range(GLA_HEADS)]

    logits = [_dot(sm_ref[r, :].astype(BF16), gup_ref[...]) for r in rows]
    parts = [_split3(jax.nn.log_sigmoid(lg + gb_ref[...]) / GLA_NORMALIZER) for lg in logits]
    bcums = [_dot(tri_bf, hi) + _dot(tri_bf, mid) + _dot(tri_bf, lo) for hi, mid, lo in parts]
    bls = []
    for r, bcum in zip(rows, bcums):
        bl = bcum[last:last + 1, :]
        q = qk_ref[r, 0:GLA_DK].astype(F32)
        k = qk_ref[r, GLA_DK:2 * GLA_DK].astype(F32)
        qe_s[r, :] = ((q * (GLA_DK_HEAD ** -0.5)) * jnp.exp(bcum)).astype(BF16)
        ke_s[r, :] = (k * jnp.exp(-bcum)).astype(BF16)
        kd_s[r, :] = (k * jnp.exp(bl - bcum)).astype(BF16)
        bls.append(bl)

    for c, r in enumerate(rows):
        for h in range(GLA_HEADS):
            att = _dot_nt(qe_s[r, heads[h]], ke_s[r, heads[h]])
            att_s[c * GLA_HEADS + h] = jnp.where(tri, att, 0.0).astype(BF16)
    for c, r in enumerate(rows):
        for h in range(GLA_HEADS):
            dS_s[c * GLA_HEADS + h] = _dot_tn(kd_s[r, heads[h]], v_ref[r, vcols[h]])

    order = list(reversed(range(nch))) if reverse else list(range(nch))
    for h in range(GLA_HEADS):
        S = S_ref[h]
        for c in order:
            Sb_s[c * GLA_HEADS + h] = S.astype(BF16)
            d = jnp.exp(bls[c][:, heads[h]])
            dcol = jnp.broadcast_to(d, (GLA_DK_HEAD, GLA_DK_HEAD)).T
            S = S * jnp.concatenate([dcol, dcol], axis=1) + dS_s[c * GLA_HEADS + h]
        S_ref[h] = S

    for c, r in enumerate(rows):
        for h in range(GLA_HEADS):
            o = _dot(att_s[c * GLA_HEADS + h], v_ref[r, vcols[h]]) + _dot(qe_s[r, heads[h]], Sb_s[c * GLA_HEADS + h])
            emit(r, vcols[h], o)


def _gla_fwd_body(qk_ref, v_ref, sm_ref, gup_ref, gb_ref, of_ref, S_ref, *scratch):
    @pl.when(pl.program_id(1) == 0)
    def _():
        S_ref[...] = jnp.zeros_like(S_ref)

    def emit(r, cols, o):
        of_ref[r, cols] = o.astype(BF16)

    _gla_block(qk_ref, v_ref, sm_ref, gup_ref, gb_ref, S_ref, *scratch, reverse=False, emit=emit)


def _gla_bwd_body(qk_ref, v_ref, sm_ref, g_ref, of_ref, gup_ref, gb_ref, nw_ref, yb_ref, S_ref, *scratch):
    @pl.when(pl.program_id(1) == 0)
    def _():
        S_ref[...] = jnp.zeros_like(S_ref)

    def emit(r, cols, ob):
        o = ob + of_ref[r, cols].astype(F32)
        o = o * lax.rsqrt(jnp.mean(o * o, axis=-1, keepdims=True) + EPS) * nw_ref[...]
        yb_ref[r, cols] = (o * _silu(g_ref[r, cols].astype(F32))).astype(BF16)

    _gla_block(qk_ref, v_ref, sm_ref, gup_ref, gb_ref, S_ref, *scratch, reverse=True, emit=emit)


def _gla(u, sm, B, L, p):
    T = B * L
    TQ = GLA_TQ
    nblk = L // TQ
    nunits = (TQ // GLA_CHUNK) * GLA_HEADS

    def rowf(b, c):
        return b * nblk + c

    def rowb(b, c):
        return b * nblk + (nblk - 1 - c)

    def col(width, colstart, rowfn):
        return pl.BlockSpec((TQ, width), lambda b, c: (rowfn(b, c), colstart // width))

    scratch = [pltpu.VMEM((GLA_HEADS, GLA_DK_HEAD, GLA_DV_HEAD), F32),
               pltpu.VMEM((TQ, GLA_DK), BF16), pltpu.VMEM((TQ, GLA_DK), BF16), pltpu.VMEM((TQ, GLA_DK), BF16),
               pltpu.VMEM((nunits, GLA_CHUNK, GLA_CHUNK), BF16),
               pltpu.VMEM((nunits, GLA_DK_HEAD, GLA_DV_HEAD), F32),
               pltpu.VMEM((nunits, GLA_DK_HEAD, GLA_DV_HEAD), BF16)]
    of = pl.pallas_call(
        _gla_fwd_body,
        grid=(B, nblk),
        in_specs=[col(1024, U_QK, rowf), col(1024, U_V, rowf), col(SMALL_W, 0, rowf),
                  _resident((LANES, GLA_DK)), _resident((1, GLA_DK))],
        out_specs=pl.BlockSpec((TQ, 1024), lambda b, c: (rowf(b, c), 0)),
        out_shape=jax.ShapeDtypeStruct((T, GLA_DV), BF16),
        scratch_shapes=scratch,
        compiler_params=_cparams(("arbitrary", "arbitrary"), 40),
        name="gla_fwd",
    )(u, u, sm, p["gup_f"], p["gb_f"])
    return pl.pallas_call(
        _gla_bwd_body,
        grid=(B, nblk),
        in_specs=[col(1024, U_QK, rowb), col(1024, U_V, rowb), col(SMALL_W, 0, rowb),
                  col(1024, U_G, rowb), pl.BlockSpec((TQ, 1024), lambda b, c: (rowb(b, c), 0)),
                  _resident((LANES, GLA_DK)), _resident((1, GLA_DK)), _resident((1, GLA_DV_HEAD))],
        out_specs=pl.BlockSpec((TQ, 1024), lambda b, c: (rowb(b, c), 0)),
        out_shape=jax.ShapeDtypeStruct((T, GLA_DV), BF16),
        scratch_shapes=scratch,
        compiler_params=_cparams(("arbitrary", "arbitrary"), 40),
        name="gla_bwd",
    )(u, u, sm, u, of, p["gup_b"], p["gb_b"], p["gla_nw"])


def _na_body(q_ref, kp_ref, kc_ref, kn_ref, vp_ref, vc_ref, vn_ref, tab_ref, o_ref, kbuf, vbuf, s_s, p_s, *, nblk):
    i = pl.program_id(1)
    TQ = NA_TQ
    rows_total = nblk * NA_ROWS
    npairs = NA_HEADS // 2
    cols = [slice(hp * LANES, (hp + 1) * LANES) for hp in range(npairs)]
    vcols = [slice(2 * hp * LANES, (2 * hp + 1) * LANES) for hp in range(npairs)]
    vones = [slice((2 * hp + 1) * LANES, (2 * hp + 2) * LANES) for hp in range(npairs)]
    vboth = [slice(2 * hp * LANES, (2 * hp + 2) * LANES) for hp in range(npairs)]

    @pl.when((pl.program_id(0) == 0) & (i == 0))
    def _():
        for hp in range(npairs):
            vbuf[:, vones[hp]] = jnp.ones((3 * TQ, LANES), BF16)

    for n, r in enumerate((kp_ref, kc_ref, kn_ref)):
        kbuf[n * TQ:(n + 1) * TQ, :] = r[...]
    for n, r in enumerate((vp_ref, vc_ref, vn_ref)):
        for hp in range(npairs):
            vbuf[n * TQ:(n + 1) * TQ, vcols[hp]] = r[:, cols[hp]]

    lane = lax.broadcasted_iota(jnp.int32, (GRID_W, LANES), 1)
    lo = lane < NA_HEAD_DIM
    nkeys = NA_KH * GRID_W

    def rows_body(t, carry):
        units = []
        for a in range(NA_RPT):
            rho = t * NA_RPT + a
            r = i * NA_ROWS + rho
            rs = jnp.clip(r - NA_KH // 2, 0, rows_total - NA_KH)
            keys = pl.ds(pl.multiple_of((rs - (i - 1) * NA_ROWS) * GRID_W, GRID_W), nkeys)
            qrows = pl.ds(pl.multiple_of(rho * GRID_W, GRID_W), GRID_W)
            units += [(a * npairs + hp, hp, r - rs, keys, qrows) for hp in range(npairs)]
        ms = []
        for n, hp, delta, keys, qrows in units:
            qp = q_ref[qrows, cols[hp]] * (NA_HEAD_DIM ** -0.5)
            zero = jnp.zeros_like(qp)
            lhs = jnp.concatenate([jnp.where(lo, qp, zero), jnp.where(lo, zero, qp)], axis=0)
            bias = jnp.concatenate(
                [tab_ref[2 * jp - delta + NA_KH - 1, hp] for jp in range(NA_KH // 2)], axis=1)
            s = _dot_nt(lhs, kbuf[keys, cols[hp]]) + bias
            s_s[n] = s
            ms.append(jnp.max(s, axis=-1, keepdims=True))
        for n, hp, delta, keys, qrows in units:
            p_s[n] = jnp.exp(s_s[n] - ms[n]).astype(BF16)
        for n, hp, delta, keys, qrows in units:
            ol = _dot(p_s[n], vbuf[keys, vboth[hp]])
            o = ol[:, 0:LANES] / ol[:, LANES:2 * LANES]
            o_ref[qrows, cols[hp]] = jnp.where(lo, o[0:GRID_W], o[GRID_W:2 * GRID_W]).astype(BF16)
        return carry

    lax.fori_loop(0, NA_ROWS // NA_RPT, rows_body, 0)


def _na(u, B, L, tab):
    T = B * L
    TQ = NA_TQ
    nblk = L // TQ
    npairs = NA_HEADS // 2

    def blk(colstart, shift):
        def im(b, i):
            return (b * nblk + jnp.clip(i + shift, 0, nblk - 1), colstart // 1024)
        return pl.BlockSpec((TQ, 1024), im)

    return pl.pallas_call(
        functools.partial(_na_body, nblk=nblk),
        grid=(B, nblk),
        in_specs=[blk(U_QC, 0), blk(U_KC, -1), blk(U_KC, 0), blk(U_KC, 1),
                  blk(U_VC, -1), blk(U_VC, 0), blk(U_VC, 1),
                  _resident(tab.shape)],
        out_specs=pl.BlockSpec((TQ, 1024), lambda b, i: (b * nblk + i, 0)),
        out_shape=jax.ShapeDtypeStruct((T, 1024), BF16),
        scratch_shapes=[pltpu.VMEM((3 * TQ, 1024), BF16), pltpu.VMEM((3 * TQ, 2 * 1024), BF16),
                        pltpu.VMEM((NA_RPT * npairs, 2 * GRID_W, NA_KH * GRID_W), F32),
                        pltpu.VMEM((NA_RPT * npairs, 2 * GRID_W, NA_KH * GRID_W), BF16)],
        compiler_params=_cparams(("arbitrary", "arbitrary"), 52),
        name="na",
    )(u, u, u, u, u, u, u, tab)


def _na_bias_table(rpb):
    c = jnp.arange(GRID_W)[:, None]
    kc = jnp.arange(GRID_W)[None, :]
    cs = jnp.clip(c - NA_KW // 2, 0, GRID_W - NA_KW)
    valid = (kc >= cs) & (kc < cs + NA_KW)
    off = jnp.clip(kc - c + NA_KW - 1, 0, 2 * NA_KW - 2)
    t = jnp.where(valid, rpb.astype(F32)[:, :, off], MASK_NEG)
    tp = jnp.concatenate([t[:, :-1], t[:, 1:]], axis=-1)
    n = 2 * NA_KH - 2
    tp = tp.reshape(NA_HEADS // 2, 2, n, GRID_W, LANES).transpose(2, 0, 1, 3, 4)
    return tp.reshape(n, NA_HEADS // 2, 2 * GRID_W, LANES)


def _merge_body(x_ref, ya_ref, yb_ref, yc_ref, g0_ref, g1_ref, g2_ref, wa_ref, wb_ref, wc_ref, wo_ref, o_ref):
    def sig(g_ref):
        return jax.nn.sigmoid(g_ref[...].astype(F32))

    m = sig(g0_ref) * _dot(ya_ref[...], wa_ref[...])
    m = m + sig(g1_ref) * _dot(yb_ref[...], wb_ref[...])
    m = m + sig(g2_ref) * _dot(yc_ref[...], wc_ref[...])
    o_ref[...] = x_ref[...] + _dot(m.astype(BF16), wo_ref[...])


def _merge(x, ya, yb, yc, u, wa, wb, wc, wo):
    T = x.shape[0]
    tok = pl.BlockSpec((MERGE_TM, D_MODEL), lambda i: (i, 0))

    def gate(k):
        return pl.BlockSpec((MERGE_TM, D_MODEL), lambda i: (i, U_G0 // 1024 + k))

    w = _resident((D_MODEL, D_MODEL))
    return pl.pallas_call(
        _merge_body,
        grid=(T // MERGE_TM,),
        in_specs=[tok, tok, tok, tok, gate(0), gate(1), gate(2), w, w, w, w],
        out_specs=tok,
        out_shape=jax.ShapeDtypeStruct((T, D_MODEL), F32),
        compiler_params=_cparams(("parallel",), 40),
        name="merge",
    )(x, ya, yb, yc, u, u, u, wa, wb, wc, wo)


def _pad_rows(w, rows):
    return jnp.pad(w, ((0, rows - w.shape[0]), (0, 0)))


def _layer_params(i, ln_ffn1_w, ffn1_w_in, ffn1_w_out, ln_mix_w, w_in, ssm_conv_w, ssm_conv_b, ssm_dt_bias,
                  ssm_a_log, ssm_d, ssm_norm_w, gla_gate_up, gla_gate_b, gla_norm_w, na_rpb,
                  w_branch_a, w_branch_b, w_branch_c, w_out, ln_ffn2_w, ffn2_w_in, ffn2_w_out):
    w = w_in[i]
    o = 0
    parts = {}
    for name, width in (("z", 1024), ("xs", 1024), ("bc", 512), ("dt", 32), ("q", 512), ("k", 512),
                        ("v", 1024), ("g", 1024), ("dn", 32), ("qc", 1024), ("kc", 1024), ("vc", 1024),
                        ("gates", 3072)):
        parts[name] = w[:, o:o + width]
        o += width
    w_small = jnp.pad(jnp.concatenate([parts["dt"], parts["dn"]], axis=1), ((0, 0), (0, SMALL_W - 64))).astype(BF16)
    w_perm = jnp.concatenate([parts[n] for n in ("z", "xs", "q", "k", "v", "g", "qc", "kc", "vc", "gates", "bc")],
                             axis=1).astype(BF16)

    def gup(d):
        full = jnp.zeros((LANES, GLA_DK), F32).at[32 + 16 * d:48 + 16 * d].set(gla_gate_up[i, d].astype(F32))
        return full.astype(BF16)

    def pad_lanes(v2):
        return jnp.broadcast_to(v2.astype(F32).reshape(2 * SSM_HEADS, 1), (2 * SSM_HEADS, LANES))

    return dict(
        ln1=ln_ffn1_w[i].reshape(1, -1), f1_in=ffn1_w_in[i].astype(BF16), f1_out=ffn1_w_out[i].astype(BF16),
        ln_mix=ln_mix_w[i].reshape(1, -1), w_perm=w_perm, w_small=w_small,
        cwx=_pad_rows(ssm_conv_w[i][:, :1024], SUBLANES), cbx=ssm_conv_b[i][:1024].reshape(1, -1),
        cwb=_pad_rows(ssm_conv_w[i][:, 1024:], SUBLANES), cbb=ssm_conv_b[i][1024:].reshape(1, -1),
        dtb=pad_lanes(ssm_dt_bias[i]), alog=pad_lanes(ssm_a_log[i]),
        dskip=jnp.repeat(ssm_d[i].astype(F32), SSM_HEAD_DIM).reshape(1, -1),
        ssm_nw=ssm_norm_w[i].reshape(1, -1),
        gup_f=gup(0), gup_b=gup(1), gb_f=gla_gate_b[i, 0].reshape(1, -1), gb_b=gla_gate_b[i, 1].reshape(1, -1),
        gla_nw=gla_norm_w[i].reshape(1, -1),
        na_tab=_na_bias_table(na_rpb[i]),
        wa=w_branch_a[i].astype(BF16), wb=w_branch_b[i].astype(BF16), wc=w_branch_c[i].astype(BF16),
        wo=w_out[i].astype(BF16),
        ln2=ln_ffn2_w[i].reshape(1, -1), f2_in=ffn2_w_in[i].astype(BF16), f2_out=ffn2_w_out[i].astype(BF16),
    )


def _trunk(x3, layers, ln_final):
    B, L, D = x3.shape
    x = x3.reshape(B * L, D)
    for li, p in enumerate(layers):
        x, h, sm = _ffn(x, p["ln1"], p["f1_in"], p["f1_out"], mix=(p["ln_mix"], p["w_small"]))
        u = _inproj(h, p["w_perm"])
        ya = _ssd(u, sm, B, L, p)
        yb = _gla(u, sm, B, L, p)
        yc = _na(u, B, L, p["na_tab"])
        x = _merge(x, ya, yb, yc, u, p["wa"], p["wb"], p["wc"], p["wo"])
        x = _ffn(x, p["ln2"], p["f2_in"], p["f2_out"], lnf=ln_final if li == len(layers) - 1 else None)
    return x.reshape(B, L, D)


def kernel(x_prompt, x_sample, ln_ffn1_w, ffn1_w_in, ffn1_w_out, ln_mix_w, w_in, ssm_conv_w, ssm_conv_b, ssm_dt_bias, ssm_a_log, ssm_d, ssm_norm_w, gla_gate_up, gla_gate_b, gla_norm_w, na_rpb, w_branch_a, w_branch_b, w_branch_c, w_out, ln_ffn2_w, ffn2_w_in, ffn2_w_out, ln_final_w):
    layers = [_layer_params(i, ln_ffn1_w, ffn1_w_in, ffn1_w_out, ln_mix_w, w_in, ssm_conv_w, ssm_conv_b,
                            ssm_dt_bias, ssm_a_log, ssm_d, ssm_norm_w, gla_gate_up, gla_gate_b, gla_norm_w,
                            na_rpb, w_branch_a, w_branch_b, w_branch_c, w_out, ln_ffn2_w, ffn2_w_in, ffn2_w_out)
              for i in range(DEPTH)]
    ln_final = ln_final_w.reshape(1, -1)
    return (_trunk(x_prompt, layers, ln_final), _trunk(x_sample, layers, ln_final))
```

```python
import functools

import jax
import jax.numpy as jnp
from jax import lax
from jax.experimental import pallas as pl
from jax.experimental.pallas import tpu as pltpu

F32 = jnp.float32
BF16 = jnp.bfloat16

D_MODEL = 1024
DEPTH = 2
GRID_W = 64
SSM_HEADS = 16
SSM_HEAD_DIM = 64
SSM_WIDTH = 1024
SSM_GROUPS = 2
SSM_STATE = 128
SSM_CONV = 5
SSM_CHUNK = 128
GLA_HEADS = 4
GLA_DK_HEAD = 128
GLA_DV_HEAD = 256
GLA_DK = 512
GLA_DV = 1024
GLA_RANK = 16
GLA_NORMALIZER = 16.0
GLA_CHUNK = 64
NA_HEADS = 16
NA_HEAD_DIM = 64
NA_KH = 8
NA_KW = 16
D_FF = 2816
EPS = 1e-6

LANES = 128
SUBLANES = 8
BF16_ROWS = 16
MASK_NEG = -1e30
LOG2E = 1.4426950408889634

U_Z, U_XS, U_QK, U_V, U_G, U_QC, U_KC, U_VC, U_G0 = (i * 1024 for i in range(9))
U_BC = 11 * 1024
U_WIDTH = U_BC + 512
U_TN = U_WIDTH // 2
SMALL_W = LANES

MXU_N = 256
FFN_TM = 1024
FFN_CHUNKS = tuple((c, min(c + 3 * MXU_N, D_FF)) for c in range(0, D_FF, 3 * MXU_N))
PROJ_TM = 512
SSD_TQ = 512
GLA_TQ = 512
NA_ROWS = 8
NA_TQ = NA_ROWS * GRID_W
NA_RPT = 4
MERGE_TM = 512


def _cparams(sem, vmem_mb):
    return pltpu.CompilerParams(dimension_semantics=sem, vmem_limit_bytes=vmem_mb * 1024 * 1024)


def _resident(shape):
    nd = len(shape)
    return pl.BlockSpec(shape, lambda *_: (0,) * nd, pipeline_mode=pl.Buffered(1))


def _rms(x, w):
    return x * lax.rsqrt(jnp.mean(x * x, axis=-1, keepdims=True) + EPS) * w


def _silu(x):
    return x * jax.nn.sigmoid(x)


def _dot(a, b):
    return jnp.dot(a, b, preferred_element_type=F32)


def _dot_nt(a, b):
    return lax.dot_general(a, b, (((1,), (1,)), ((), ())), preferred_element_type=F32)


def _dot_tn(a, b):
    return lax.dot_general(a, b, (((0,), (0,)), ((), ())), preferred_element_type=F32)


def _split3(x):
    hi = x.astype(BF16)
    r1 = x - hi.astype(F32)
    mid = r1.astype(BF16)
    lo = (r1 - mid.astype(F32)).astype(BF16)
    return hi, mid, lo


def _lane_cumsum(x, reverse):
    n = x.shape[-1]
    lane = lax.broadcasted_iota(jnp.int32, x.shape, x.ndim - 1)
    s = 1
    while s < n:
        if reverse:
            x = x + jnp.where(lane < n - s, pltpu.roll(x, n - s, x.ndim - 1), 0.0)
        else:
            x = x + jnp.where(lane >= s, pltpu.roll(x, s, x.ndim - 1), 0.0)
        s *= 2
    return x


def _cumsum_dot(tri_bf, x):
    hi, mid, lo = _split3(x)
    return _dot(tri_bf, hi) + _dot(tri_bf, mid) + _dot(tri_bf, lo)


def _ffn_body(x_ref, lnw_ref, win_ref, wout_ref, *rest, final_norm, emit_mix):
    rest = list(rest)
    lnf_ref = rest.pop(0) if final_norm else None
    if emit_mix:
        lnm_ref, wsm_ref = rest.pop(0), rest.pop(0)
    o_ref = rest.pop(0)
    x = x_ref[...]
    h = _rms(x, lnw_ref[...]).astype(BF16)
    acc = None
    for c0, c1 in FFN_CHUNKS:
        a = _dot(h, win_ref[:, c0:c1])
        b = _dot(h, win_ref[:, D_FF + c0:D_FF + c1])
        p = _dot((_silu(a) * b).astype(BF16), wout_ref[c0:c1, :])
        acc = p if acc is None else acc + p
    y = x + 0.5 * acc
    if final_norm:
        y = _rms(y, lnf_ref[...])
    o_ref[...] = y
    if emit_mix:
        hm_ref, sm_ref = rest
        hm = _rms(y, lnm_ref[...]).astype(BF16)
        hm_ref[...] = hm
        sm_ref[...] = _dot(hm, wsm_ref[...])


def _ffn(x, lnw, w_in, w_out, lnf=None, mix=None):
    T = x.shape[0]
    tok = pl.BlockSpec((FFN_TM, D_MODEL), lambda i: (i, 0))
    in_specs = [tok, _resident((1, D_MODEL)), _resident((D_MODEL, 2 * D_FF)), _resident((D_FF, D_MODEL))]
    args = [x, lnw, w_in, w_out]
    out_specs = [tok]
    out_shape = [jax.ShapeDtypeStruct((T, D_MODEL), F32)]
    if lnf is not None:
        in_specs.append(_resident((1, D_MODEL)))
        args.append(lnf)
    if mix is not None:
        in_specs += [_resident((1, D_MODEL)), _resident((D_MODEL, SMALL_W))]
        args += list(mix)
        out_specs += [tok, pl.BlockSpec((FFN_TM, SMALL_W), lambda i: (i, 0))]
        out_shape += [jax.ShapeDtypeStruct((T, D_MODEL), BF16), jax.ShapeDtypeStruct((T, SMALL_W), F32)]
    out = pl.pallas_call(
        functools.partial(_ffn_body, final_norm=lnf is not None, emit_mix=mix is not None),
        grid=(T // FFN_TM,),
        in_specs=in_specs,
        out_specs=out_specs,
        out_shape=out_shape,
        compiler_params=_cparams(("parallel",), 56),
        name="ffn",
    )(*args)
    return out if mix is not None else out[0]


def _inproj_body(h_ref, w_ref, u_ref):
    u_ref[...] = _dot(h_ref[...], w_ref[...]).astype(BF16)


def _inproj(h, w):
    T = h.shape[0]
    return pl.pallas_call(
        _inproj_body,
        grid=(U_WIDTH // U_TN, T // PROJ_TM),
        in_specs=[pl.BlockSpec((PROJ_TM, D_MODEL), lambda j, i: (i, 0)),
                  pl.BlockSpec((D_MODEL, U_TN), lambda j, i: (0, j))],
        out_specs=pl.BlockSpec((PROJ_TM, U_TN), lambda j, i: (i, j)),
        out_shape=jax.ShapeDtypeStruct((T, U_WIDTH), BF16),
        compiler_params=_cparams(("arbitrary", "arbitrary"), 48),
        name="inproj",
    )(h, w)


def _ssd_decay_terms(sm_ref, dtb, alog, nch, reverse):
    Q = SSM_CHUNK
    off = SSM_HEADS if reverse else 0
    last = 0 if reverse else Q - 1
    raw = jnp.concatenate([sm_ref[c * Q:(c + 1) * Q, :].T[off:off + SSM_HEADS, :] for c in range(nch)], axis=0)
    tile = lambda p: jnp.concatenate([p[off:off + SSM_HEADS, :]] * nch, axis=0)
    dtT = jax.nn.softplus(raw + tile(dtb))
    kk = lax.broadcasted_iota(jnp.int32, (Q, Q), 0)
    jj = lax.broadcasted_iota(jnp.int32, (Q, Q), 1)
    triT = ((kk >= jj) if reverse else (kk <= jj)).astype(BF16)
    hi, mid, lo = _split3(dtT * (-jnp.exp(tile(alog))))
    csT = _dot(hi, triT) + _dot(mid, triT) + _dot(lo, triT)
    r2T = csT * LOG2E - jnp.log2(dtT)
    tot = csT[:, last:last + 1]
    WT = jnp.exp(tot - csT) * dtT
    elT = jnp.broadcast_to(jnp.exp(tot), csT.shape)
    cs = jnp.concatenate([csT, jnp.zeros((Q - nch * SSM_HEADS, Q), F32)], axis=0).T
    return r2T, WT, elT, cs * LOG2E, jnp.exp(cs)


def _ssd_chunk(x_get, Bc, Cc, terms, c, S_ref, reverse):
    Q = SSM_CHUNK
    r2T, WT, elT, c2, ecs = terms
    mxu_bcast = stack_pair = reverse
    ii = lax.broadcasted_iota(jnp.int32, (Q, Q), 0)
    jj = lax.broadcasted_iota(jnp.int32, (Q, Q), 1)
    tri = (ii <= jj) if reverse else (ii >= jj)
    lo = jj < SSM_HEAD_DIM
    if mxu_bcast:
        kk = lax.broadcasted_iota(jnp.int32, (Q, SSM_HEADS * LANES), 0)
        nn = lax.broadcasted_iota(jnp.int32, (Q, SSM_HEADS * LANES), 1)
        sel = jnp.where(kk == c * SSM_HEADS + jnp.right_shift(nn, 7), 1.0, 0.0).astype(BF16)
        eb = _dot(ecs.astype(BF16), sel)
    ys = []
    for g in range(SSM_GROUPS):
        Bg = Bc[:, g * SSM_STATE:(g + 1) * SSM_STATE]
        Cg = Cc[:, g * SSM_STATE:(g + 1) * SSM_STATE]
        cb = _dot_nt(Cg.astype(BF16), Bg.astype(BF16))
        BgT = Bg.T
        for pp in range(SSM_HEADS // SSM_GROUPS // 2):
            hp = g * (SSM_HEADS // SSM_GROUPS // 2) + pp
            xb = x_get(hp)
            Sp = S_ref[hp]
            rhs = jnp.concatenate([xb, Sp.astype(BF16)], axis=0)
            lhs, wts, scs, outs, dSs = [], [], [], [], []
            for e in range(2):
                h = c * SSM_HEADS + 2 * hp + e
                Lm = jnp.where(tri, jnp.exp2(c2[:, h:h + 1] - r2T[h:h + 1, :]), 0.0)
                M = (cb * Lm).astype(BF16)
                if mxu_bcast:
                    ecol = eb[:, (2 * hp + e) * LANES:(2 * hp + e + 1) * LANES]
                else:
                    ecol = ecs[:, h:h + 1]
                Ce = (Cg * ecol).astype(BF16)
                lhs.append(jnp.concatenate([M, Ce], axis=1))
                wts.append((BgT * WT[h:h + 1, :]).astype(BF16))
                scs.append(elT[h:h + 1, :])
                if not stack_pair:
                    outs.append(_dot(lhs[e], rhs))
                    dSs.append(_dot(wts[e], xb))
            if stack_pair:
                out = _dot(jnp.concatenate(lhs, axis=0), rhs)
                dS = _dot(jnp.concatenate(wts, axis=0), xb)
                outs, dSs = (out[0:Q], out[Q:2 * Q]), (dS[0:Q], dS[Q:2 * Q])
            ys.append(jnp.where(lo, outs[0], outs[1]))
            S_ref[hp] = Sp * jnp.where(lo, scs[0], scs[1]) + jnp.where(lo, dSs[0], dSs[1])
    return ys


CONV_WIN = SSM_CHUNK + 2 * BF16_ROWS
CONV_SHIFTS = tuple(k for k in range(SSM_CONV) if k != SSM_CONV // 2)


def _conv_shift_matrix():
    t = lax.broadcasted_iota(jnp.int32, (len(CONV_SHIFTS) * SSM_CHUNK, CONV_WIN), 0)
    j = lax.broadcasted_iota(jnp.int32, (len(CONV_SHIFTS) * SSM_CHUNK, CONV_WIN), 1)
    hit = None
    for n, k in enumerate(CONV_SHIFTS):
        m = (t >= n * SSM_CHUNK) & (t < (n + 1) * SSM_CHUNK) & (
            j == t - n * SSM_CHUNK + BF16_ROWS - SSM_CONV // 2 + k)
        hit = m if hit is None else hit | m
    return jnp.where(hit, 1.0, 0.0).astype(BF16)


def _conv_silu_rows(ext_ref, shift, w_ref, b_ref, r0, c0, cw):
    win = ext_ref[r0:r0 + CONV_WIN, c0:c0 + cw]
    mid = SSM_CONV // 2
    acc = b_ref[:, c0:c0 + cw] + win[BF16_ROWS:BF16_ROWS + SSM_CHUNK].astype(F32) * w_ref[mid:mid + 1, c0:c0 + cw]
    shifted = _dot(shift, win)
    for n, k in enumerate(CONV_SHIFTS):
        acc = acc + shifted[n * SSM_CHUNK:(n + 1) * SSM_CHUNK] * w_ref[k:k + 1, c0:c0 + cw]
    return _silu(acc)


def _ssd_fwd_body(xs_ref, xsp_ref, xsn_ref, bc_ref, bcp_ref, bcn_ref, sm_ref,
                  cwx_ref, cbx_ref, cwb_ref, cbb_ref, dtb_ref, alog_ref,
                  yf_ref, xa_ref, ba_ref, extx, extb, S_ref, *, nblk):
    c = pl.program_id(1)
    TQ = SSD_TQ
    H = BF16_ROWS

    @pl.when(c == 0)
    def _():
        S_ref[...] = jnp.zeros_like(S_ref)

    first = c == 0
    final = c == nblk - 1
    for ext, cur, prv, nxt in ((extx, xs_ref, xsp_ref, xsn_ref), (extb, bc_ref, bcp_ref, bcn_ref)):
        ext[0:H, :] = jnp.where(first, jnp.zeros_like(prv[...]), prv[...])
        ext[H:H + TQ, :] = cur[...]
        ext[H + TQ:2 * H + TQ, :] = jnp.where(final, jnp.zeros_like(nxt[...]), nxt[...])

    shift = _conv_shift_matrix()
    terms = _ssd_decay_terms(sm_ref, dtb_ref[...], alog_ref[...], TQ // SSM_CHUNK, reverse=False)
    for q in range(TQ // SSM_CHUNK):
        r0 = q * SSM_CHUNK
        rows = slice(r0, r0 + SSM_CHUNK)
        for c0 in range(0, SSM_WIDTH, MXU_N):
            xa_ref[rows, c0:c0 + MXU_N] = _conv_silu_rows(extx, shift, cwx_ref, cbx_ref, r0, c0, MXU_N).astype(BF16)
        for c0 in range(0, 512, MXU_N):
            ba_ref[rows, c0:c0 + MXU_N] = _conv_silu_rows(extb, shift, cwb_ref, cbb_ref, r0, c0, MXU_N).astype(BF16)
        ys = _ssd_chunk(lambda hp: xa_ref[rows, hp * LANES:(hp + 1) * LANES],
                        ba_ref[rows, 0:256].astype(F32), ba_ref[rows, 256:512].astype(F32),
                        terms, q, S_ref, reverse=False)
        for hp, y in enumerate(ys):
            yf_ref[rows, hp * LANES:(hp + 1) * LANES] = y.astype(BF16)


def _ssd_bwd_body(xa_ref, ba_ref, sm_ref, z_ref, yf_ref, dtb_ref, alog_ref, dsk_ref, nw_ref,
                  ya_ref, S_ref):
    TQ = SSD_TQ

    @pl.when(pl.program_id(1) == 0)
    def _():
        S_ref[...] = jnp.zeros_like(S_ref)

    half = SSM_WIDTH // SSM_GROUPS
    terms = _ssd_decay_terms(sm_ref, dtb_ref[...], alog_ref[...], TQ // SSM_CHUNK, reverse=True)
    for q in reversed(range(TQ // SSM_CHUNK)):
        rows = slice(q * SSM_CHUNK, (q + 1) * SSM_CHUNK)
        ys = _ssd_chunk(lambda hp: xa_ref[rows, hp * LANES:(hp + 1) * LANES],
                        ba_ref[rows, 0:256].astype(F32), ba_ref[rows, 256:512].astype(F32),
                        terms, q, S_ref, reverse=True)
        for g in range(SSM_GROUPS):
            ts = []
            ssq = None
            for pp in range(4):
                hp = g * 4 + pp
                cols = slice(hp * LANES, (hp + 1) * LANES)
                t = ys[hp] + yf_ref[rows, cols].astype(F32) + dsk_ref[:, cols] * xa_ref[rows, cols].astype(F32)
                t = t * _silu(z_ref[rows, cols].astype(F32))
                ts.append(t)
                s = jnp.sum(t * t, axis=-1, keepdims=True)
                ssq = s if ssq is None else ssq + s
            inv = lax.rsqrt(ssq * (1.0 / half) + EPS)
            for pp in range(4):
                cols = slice((g * 4 + pp) * LANES, (g * 4 + pp + 1) * LANES)
                ya_ref[rows, cols] = (ts[pp] * inv * nw_ref[:, cols]).astype(BF16)


def _ssd(u, sm, B, L, p):
    T = B * L
    TQ = SSD_TQ
    nblk = L // TQ
    hb = TQ // BF16_ROWS
    nh = T // BF16_ROWS

    def rowf(b, c):
        return b * nblk + c

    def rowb(b, c):
        return b * nblk + (nblk - 1 - c)

    def col(width, colstart, rowfn):
        return pl.BlockSpec((TQ, width), lambda b, c: (rowfn(b, c), colstart // width))

    def halo(width, colstart, shift):
        def im(b, c):
            r = (b * nblk + c + shift) * hb - (1 - shift)
            return (jnp.clip(r, 0, nh - 1), colstart // width)
        return pl.BlockSpec((BF16_ROWS, width), im)

    state = pltpu.VMEM((SSM_HEADS // 2, SSM_STATE, LANES), F32)
    yf, xa, ba = pl.pallas_call(
        functools.partial(_ssd_fwd_body, nblk=nblk),
        grid=(B, nblk),
        in_specs=[col(1024, U_XS, rowf), halo(1024, U_XS, 0), halo(1024, U_XS, 1),
                  col(512, U_BC, rowf), halo(512, U_BC, 0), halo(512, U_BC, 1),
                  col(SMALL_W, 0, rowf),
                  _resident((SUBLANES, 1024)), _resident((1, 1024)),
                  _resident((SUBLANES, 512)), _resident((1, 512)),
                  _resident((2 * SSM_HEADS, LANES)), _resident((2 * SSM_HEADS, LANES))],
        out_specs=[pl.BlockSpec((TQ, 1024), lambda b, c: (rowf(b, c), 0)),
                   pl.BlockSpec((TQ, 1024), lambda b, c: (rowf(b, c), 0)),
                   pl.BlockSpec((TQ, 512), lambda b, c: (rowf(b, c), 0))],
        out_shape=[jax.ShapeDtypeStruct((T, 1024), BF16), jax.ShapeDtypeStruct((T, 1024), BF16),
                   jax.ShapeDtypeStruct((T, 512), BF16)],
        scratch_shapes=[pltpu.VMEM((TQ + 2 * BF16_ROWS, 1024), BF16),
                        pltpu.VMEM((TQ + 2 * BF16_ROWS, 512), BF16), state],
        compiler_params=_cparams(("arbitrary", "arbitrary"), 40),
        name="ssd_fwd",
    )(u, u, u, u, u, u, sm, p["cwx"], p["cbx"], p["cwb"], p["cbb"], p["dtb"], p["alog"])

    def rb(width):
        return pl.BlockSpec((TQ, width), lambda b, c: (rowb(b, c), 0))

    return pl.pallas_call(
        _ssd_bwd_body,
        grid=(B, nblk),
        in_specs=[rb(1024), rb(512), col(SMALL_W, 0, rowb), col(1024, U_Z, rowb), rb(1024),
                  _resident((2 * SSM_HEADS, LANES)), _resident((2 * SSM_HEADS, LANES)),
                  _resident((1, 1024)), _resident((1, 1024))],
        out_specs=rb(1024),
        out_shape=jax.ShapeDtypeStruct((T, 1024), BF16),
        scratch_shapes=[state],
        compiler_params=_cparams(("arbitrary", "arbitrary"), 40),
        name="ssd_bwd",
    )(xa, ba, sm, u, yf, p["dtb"], p["alog"], p["dskip"], p["ssm_nw"])


def _gla_block(qk_ref, v_ref, sm_ref, gup_ref, gb_ref, S_ref, qe_s, ke_s, kd_s, att_s, dS_s, Sb_s, reverse, emit):
    Q = GLA_CHUNK
    nch = GLA_TQ // Q
    last = 0 if reverse else Q - 1
    ii = lax.broadcasted_iota(jnp.int32, (Q, Q), 0)
    jj = lax.broadcasted_iota(jnp.int32, (Q, Q), 1)
    tri = (ii <= jj) if reverse else (ii >= jj)
    tri_bf = tri.astype(BF16)
    rows = [slice(c * Q, (c + 1) * Q) for c in range(nch)]
    heads = [slice(h * GLA_DK_HEAD, (h + 1) * GLA_DK_HEAD) for h in range(GLA_HEADS)]
    vcols = [slice(h * GLA_DV_HEAD, (h + 1) * GLA_DV_HEAD) for h in range(GLA_HEADS)]

    logits = [_dot(sm_ref[r, :].astype(BF16), gup_ref[...]) for r in rows]
    parts = [_split3(jax.nn.log_sigmoid(lg + gb_ref[...]) / GLA_NORMALIZER) for lg in logits]
    bcums = [_dot(tri_bf, hi) + _dot(tri_bf, mid) + _dot(tri_bf, lo) for hi, mid, lo in parts]
    bls = []
    for r, bcum in zip(rows, bcums):
        bl = bcum[last:last + 1, :]
        q = qk_ref[r, 0:GLA_DK].astype(F32)
        k = qk_ref[r, GLA_DK:2 * GLA_DK].astype(F32)
        qe_s[r, :] = ((q * (GLA_DK_HEAD ** -0.5)) * jnp.exp(bcum)).astype(BF16)
        ke_s[r, :] = (k * jnp.exp(-bcum)).astype(BF16)
        kd_s[r, :] = (k * jnp.exp(bl - bcum)).astype(BF16)
        bls.append(bl)

    for c, r in enumerate(rows):
        for h in range(GLA_HEADS):
            att = _dot_nt(qe_s[r, heads[h]], ke_s[r, heads[h]])
            att_s[c * GLA_HEADS + h] = jnp.where(tri, att, 0.0).astype(BF16)
    for c, r in enumerate(rows):
        for h in range(GLA_HEADS):
            dS_s[c * GLA_HEADS + h] = _dot_tn(kd_s[r, heads[h]], v_ref[r, vcols[h]])

    order = list(reversed(range(nch))) if reverse else list(range(nch))
    for h in range(GLA_HEADS):
        S = S_ref[h]
        for c in order:
            Sb_s[c * GLA_HEADS + h] = S.astype(BF16)
            d = jnp.exp(bls[c][:, heads[h]])
            dcol = jnp.broadcast_to(d, (GLA_DK_HEAD, GLA_DK_HEAD)).T
            S = S * jnp.concatenate([dcol, dcol], axis=1) + dS_s[c * GLA_HEADS + h]
        S_ref[h] = S

    for c, r in enumerate(rows):
        for h in range(GLA_HEADS):
            o = _dot(att_s[c * GLA_HEADS + h], v_ref[r, vcols[h]]) + _dot(qe_s[r, heads[h]], Sb_s[c * GLA_HEADS + h])
            emit(r, vcols[h], o)


def _gla_fwd_body(qk_ref, v_ref, sm_ref, gup_ref, gb_ref, of_ref, S_ref, *scratch):
    @pl.when(pl.program_id(1) == 0)
    def _():
        S_ref[...] = jnp.zeros_like(S_ref)

    def emit(r, cols, o):
        of_ref[r, cols] = o.astype(BF16)

    _gla_block(qk_ref, v_ref, sm_ref, gup_ref, gb_ref, S_ref, *scratch, reverse=False, emit=emit)


def _gla_bwd_body(qk_ref, v_ref, sm_ref, g_ref, of_ref, gup_ref, gb_ref, nw_ref, yb_ref, S_ref, *scratch):
    @pl.when(pl.program_id(1) == 0)
    def _():
        S_ref[...] = jnp.zeros_like(S_ref)

    def emit(r, cols, ob):
        o = ob + of_ref[r, cols].astype(F32)
        o = o * lax.rsqrt(jnp.mean(o * o, axis=-1, keepdims=True) + EPS) * nw_ref[...]
        yb_ref[r, cols] = (o * _silu(g_ref[r, cols].astype(F32))).astype(BF16)

    _gla_block(qk_ref, v_ref, sm_ref, gup_ref, gb_ref, S_ref, *scratch, reverse=True, emit=emit)


def _gla(u, sm, B, L, p):
    T = B * L
    TQ = GLA_TQ
    nblk = L // TQ
    nunits = (TQ // GLA_CHUNK) * GLA_HEADS

    def rowf(b, c):
        return b * nblk + c

    def rowb(b, c):
        return b * nblk + (nblk - 1 - c)

    def col(width, colstart, rowfn):
        return pl.BlockSpec((TQ, width), lambda b, c: (rowfn(b, c), colstart // width))

    scratch = [pltpu.VMEM((GLA_HEADS, GLA_DK_HEAD, GLA_DV_HEAD), F32),
               pltpu.VMEM((TQ, GLA_DK), BF16), pltpu.VMEM((TQ, GLA_DK), BF16), pltpu.VMEM((TQ, GLA_DK), BF16),
               pltpu.VMEM((nunits, GLA_CHUNK, GLA_CHUNK), BF16),
               pltpu.VMEM((nunits, GLA_DK_HEAD, GLA_DV_HEAD), F32),
               pltpu.VMEM((nunits, GLA_DK_HEAD, GLA_DV_HEAD), BF16)]
    of = pl.pallas_call(
        _gla_fwd_body,
        grid=(B, nblk),
        in_specs=[col(1024, U_QK, rowf), col(1024, U_V, rowf), col(SMALL_W, 0, rowf),
                  _resident((LANES, GLA_DK)), _resident((1, GLA_DK))],
        out_specs=pl.BlockSpec((TQ, 1024), lambda b, c: (rowf(b, c), 0)),
        out_shape=jax.ShapeDtypeStruct((T, GLA_DV), BF16),
        scratch_shapes=scratch,
        compiler_params=_cparams(("arbitrary", "arbitrary"), 40),
        name="gla_fwd",
    )(u, u, sm, p["gup_f"], p["gb_f"])
    return pl.pallas_call(
        _gla_bwd_body,
        grid=(B, nblk),
        in_specs=[col(1024, U_QK, rowb), col(1024, U_V, rowb), col(SMALL_W, 0, rowb),
                  col(1024, U_G, rowb), pl.BlockSpec((TQ, 1024), lambda b, c: (rowb(b, c), 0)),
                  _resident((LANES, GLA_DK)), _resident((1, GLA_DK)), _resident((1, GLA_DV_HEAD))],
        out_specs=pl.BlockSpec((TQ, 1024), lambda b, c: (rowb(b, c), 0)),
        out_shape=jax.ShapeDtypeStruct((T, GLA_DV), BF16),
        scratch_shapes=scratch,
        compiler_params=_cparams(("arbitrary", "arbitrary"), 40),
        name="gla_bwd",
    )(u, u, sm, u, of, p["gup_b"], p["gb_b"], p["gla_nw"])


def _na_body(q_ref, kp_ref, kc_ref, kn_ref, vp_ref, vc_ref, vn_ref, tab_ref, o_ref, kbuf, vbuf, s_s, p_s, *, nblk):
    i = pl.program_id(1)
    TQ = NA_TQ
    rows_total = nblk * NA_ROWS
    npairs = NA_HEADS // 2
    cols = [slice(hp * LANES, (hp + 1) * LANES) for hp in range(npairs)]
    vcols = [slice(2 * hp * LANES, (2 * hp + 1) * LANES) for hp in range(npairs)]
    vones = [slice((2 * hp + 1) * LANES, (2 * hp + 2) * LANES) for hp in range(npairs)]
    vboth = [slice(2 * hp * LANES, (2 * hp + 2) * LANES) for hp in range(npairs)]

    @pl.when((pl.program_id(0) == 0) & (i == 0))
    def _():
        for hp in range(npairs):
            vbuf[:, vones[hp]] = jnp.ones((3 * TQ, LANES), BF16)

    for n, r in enumerate((kp_ref, kc_ref, kn_ref)):
        kbuf[n * TQ:(n + 1) * TQ, :] = r[...]
    for n, r in enumerate((vp_ref, vc_ref, vn_ref)):
        for hp in range(npairs):
            vbuf[n * TQ:(n + 1) * TQ, vcols[hp]] = r[:, cols[hp]]

    lane = lax.broadcasted_iota(jnp.int32, (GRID_W, LANES), 1)
    lo = lane < NA_HEAD_DIM
    nkeys = NA_KH * GRID_W

    def rows_body(t, carry):
        units = []
        for a in range(NA_RPT):
            rho = t * NA_RPT + a
            r = i * NA_ROWS + rho
            rs = jnp.clip(r - NA_KH // 2, 0, rows_total - NA_KH)
            keys = pl.ds(pl.multiple_of((rs - (i - 1) * NA_ROWS) * GRID_W, GRID_W), nkeys)
            qrows = pl.ds(pl.multiple_of(rho * GRID_W, GRID_W), GRID_W)
            units += [(a * npairs + hp, hp, r - rs, keys, qrows) for hp in range(npairs)]
        ms = []
        for n, hp, delta, keys, qrows in units:
            qp = q_ref[qrows, cols[hp]] * (NA_HEAD_DIM ** -0.5)
            zero = jnp.zeros_like(qp)
            lhs = jnp.concatenate([jnp.where(lo, qp, zero), jnp.where(lo, zero, qp)], axis=0)
            bias = jnp.concatenate(
                [tab_ref[2 * jp - delta + NA_KH - 1, hp] for jp in range(NA_KH // 2)], axis=1)
            s = _dot_nt(lhs, kbuf[keys, cols[hp]]) + bias
            s_s[n] = s
            ms.append(jnp.max(s, axis=-1, keepdims=True))
        for n, hp, delta, keys, qrows in units:
            p_s[n] = jnp.exp(s_s[n] - ms[n]).astype(BF16)
        for n, hp, delta, keys, qrows in units:
            ol = _dot(p_s[n], vbuf[keys, vboth[hp]])
            o = ol[:, 0:LANES] / ol[:, LANES:2 * LANES]
            o_ref[qrows, cols[hp]] = jnp.where(lo, o[0:GRID_W], o[GRID_W:2 * GRID_W]).astype(BF16)
        return carry

    lax.fori_loop(0, NA_ROWS // NA_RPT, rows_body, 0)


def _na(u, B, L, tab):
    T = B * L
    TQ = NA_TQ
    nblk = L // TQ
    npairs = NA_HEADS // 2

    def blk(colstart, shift):
        def im(b, i):
            return (b * nblk + jnp.clip(i + shift, 0, nblk - 1), colstart // 1024)
        return pl.BlockSpec((TQ, 1024), im)

    return pl.pallas_call(
        functools.partial(_na_body, nblk=nblk),
        grid=(B, nblk),
        in_specs=[blk(U_QC, 0), blk(U_KC, -1), blk(U_KC, 0), blk(U_KC, 1),
                  blk(U_VC, -1), blk(U_VC, 0), blk(U_VC, 1),
                  _resident(tab.shape)],
        out_specs=pl.BlockSpec((TQ, 1024), lambda b, i: (b * nblk + i, 0)),
        out_shape=jax.ShapeDtypeStruct((T, 1024), BF16),
        scratch_shapes=[pltpu.VMEM((3 * TQ, 1024), BF16), pltpu.VMEM((3 * TQ, 2 * 1024), BF16),
                        pltpu.VMEM((NA_RPT * npairs, 2 * GRID_W, NA_KH * GRID_W), F32),
                        pltpu.VMEM((NA_RPT * npairs, 2 * GRID_W, NA_KH * GRID_W), BF16)],
        compiler_params=_cparams(("arbitrary", "arbitrary"), 52),
        name="na",
    )(u, u, u, u, u, u, u, tab)


def _na_bias_table(rpb):
    c = jnp.arange(GRID_W)[:, None]
    kc = jnp.arange(GRID_W)[None, :]
    cs = jnp.clip(c - NA_KW // 2, 0, GRID_W - NA_KW)
    valid = (kc >= cs) & (kc < cs + NA_KW)
    off = jnp.clip(kc - c + NA_KW - 1, 0, 2 * NA_KW - 2)
    t = jnp.where(valid, rpb.astype(F32)[:, :, off], MASK_NEG)
    tp = jnp.concatenate([t[:, :-1], t[:, 1:]], axis=-1)
    n = 2 * NA_KH - 2
    tp = tp.reshape(NA_HEADS // 2, 2, n, GRID_W, LANES).transpose(2, 0, 1, 3, 4)
    return tp.reshape(n, NA_HEADS // 2, 2 * GRID_W, LANES)


def _merge_body(x_ref, ya_ref, yb_ref, yc_ref, g0_ref, g1_ref, g2_ref, wa_ref, wb_ref, wc_ref, wo_ref, o_ref):
    def sig(g_ref):
        return jax.nn.sigmoid(g_ref[...].astype(F32))

    m = sig(g0_ref) * _dot(ya_ref[...], wa_ref[...])
    m = m + sig(g1_ref) * _dot(yb_ref[...], wb_ref[...])
    m = m + sig(g2_ref) * _dot(yc_ref[...], wc_ref[...])
    o_ref[...] = x_ref[...] + _dot(m.astype(BF16), wo_ref[...])


def _merge(x, ya, yb, yc, u, wa, wb, wc, wo):
    T = x.shape[0]
    tok = pl.BlockSpec((MERGE_TM, D_MODEL), lambda i: (i, 0))

    def gate(k):
        return pl.BlockSpec((MERGE_TM, D_MODEL), lambda i: (i, U_G0 // 1024 + k))

    w = _resident((D_MODEL, D_MODEL))
    return pl.pallas_call(
        _merge_body,
        grid=(T // MERGE_TM,),
        in_specs=[tok, tok, tok, tok, gate(0), gate(1), gate(2), w, w, w, w],
        out_specs=tok,
        out_shape=jax.ShapeDtypeStruct((T, D_MODEL), F32),
        compiler_params=_cparams(("parallel",), 40),
        name="merge",
    )(x, ya, yb, yc, u, u, u, wa, wb, wc, wo)


def _pad_rows(w, rows):
    return jnp.pad(w, ((0, rows - w.shape[0]), (0, 0)))


def _layer_params(i, ln_ffn1_w, ffn1_w_in, ffn1_w_out, ln_mix_w, w_in, ssm_conv_w, ssm_conv_b, ssm_dt_bias,
                  ssm_a_log, ssm_d, ssm_norm_w, gla_gate_up, gla_gate_b, gla_norm_w, na_rpb,
                  w_branch_a, w_branch_b, w_branch_c, w_out, ln_ffn2_w, ffn2_w_in, ffn2_w_out):
    w = w_in[i]
    o = 0
    parts = {}
    for name, width in (("z", 1024), ("xs", 1024), ("bc", 512), ("dt", 32), ("q", 512), ("k", 512),
                        ("v", 1024), ("g", 1024), ("dn", 32), ("qc", 1024), ("kc", 1024), ("vc", 1024),
                        ("gates", 3072)):
        parts[name] = w[:, o:o + width]
        o += width
    w_small = jnp.pad(jnp.concatenate([parts["dt"], parts["dn"]], axis=1), ((0, 0), (0, SMALL_W - 64))).astype(BF16)
    w_perm = jnp.concatenate([parts[n] for n in ("z", "xs", "q", "k", "v", "g", "qc", "kc", "vc", "gates", "bc")],
                             axis=1).astype(BF16)

    def gup(d):
        full = jnp.zeros((LANES, GLA_DK), F32).at[32 + 16 * d:48 + 16 * d].set(gla_gate_up[i, d].astype(F32))
        return full.astype(BF16)

    def pad_lanes(v2):
        return jnp.broadcast_to(v2.astype(F32).reshape(2 * SSM_HEADS, 1), (2 * SSM_HEADS, LANES))

    return dict(
        ln1=ln_ffn1_w[i].reshape(1, -1), f1_in=ffn1_w_in[i].astype(BF16), f1_out=ffn1_w_out[i].astype(BF16),
        ln_mix=ln_mix_w[i].reshape(1, -1), w_perm=w_perm, w_small=w_small,
        cwx=_pad_rows(ssm_conv_w[i][:, :1024], SUBLANES), cbx=ssm_conv_b[i][:1024].reshape(1, -1),
        cwb=_pad_rows(ssm_conv_w[i][:, 1024:], SUBLANES), cbb=ssm_conv_b[i][1024:].reshape(1, -1),
        dtb=pad_lanes(ssm_dt_bias[i]), alog=pad_lanes(ssm_a_log[i]),
        dskip=jnp.repeat(ssm_d[i].astype(F32), SSM_HEAD_DIM).reshape(1, -1),
        ssm_nw=ssm_norm_w[i].reshape(1, -1),
        gup_f=gup(0), gup_b=gup(1), gb_f=gla_gate_b[i, 0].reshape(1, -1), gb_b=gla_gate_b[i, 1].reshape(1, -1),
        gla_nw=gla_norm_w[i].reshape(1, -1),
        na_tab=_na_bias_table(na_rpb[i]),
        wa=w_branch_a[i].astype(BF16), wb=w_branch_b[i].astype(BF16), wc=w_branch_c[i].astype(BF16),
        wo=w_out[i].astype(BF16),
        ln2=ln_ffn2_w[i].reshape(1, -1), f2_in=ffn2_w_in[i].astype(BF16), f2_out=ffn2_w_out[i].astype(BF16),
    )


def _trunk(x3, layers, ln_final):
    B, L, D = x3.shape
    x = x3.reshape(B * L, D)
    for li, p in enumerate(layers):
        x, h, sm = _ffn(x, p["ln1"], p["f1_in"], p["f1_out"], mix=(p["ln_mix"], p["w_small"]))
        u = _inproj(h, p["w_perm"])
        ya = _ssd(u, sm, B, L, p)
        yb = _gla(u, sm, B, L, p)
        yc = _na(u, B, L, p["na_tab"])
        x = _merge(x, ya, yb, yc, u, p["wa"], p["wb"], p["wc"], p["wo"])
        x = _ffn(x, p["ln2"], p["f2_in"], p["f2_out"], lnf=ln_final if li == len(layers) - 1 else None)
    return x.reshape(B, L, D)


def kernel(x_prompt, x_sample, ln_ffn1_w, ffn1_w_in, ffn1_w_out, ln_mix_w, w_in, ssm_conv_w, ssm_conv_b, ssm_dt_bias, ssm_a_log, ssm_d, ssm_norm_w, gla_gate_up, gla_gate_b, gla_norm_w, na_rpb, w_branch_a, w_branch_b, w_branch_c, w_out, ln_ffn2_w, ffn2_w_in, ffn2_w_out, ln_final_w):
    layers = [_layer_params(i, ln_ffn1_w, ffn1_w_in, ffn1_w_out, ln_mix_w, w_in, ssm_conv_w, ssm_conv_b,
                            ssm_dt_bias, ssm_a_log, ssm_d, ssm_norm_w, gla_gate_up, gla_gate_b, gla_norm_w,
                            na_rpb, w_branch_a, w_branch_b, w_branch_c, w_out, ln_ffn2_w, ffn2_w_in, ffn2_w_out)
              for i in range(DEPTH)]
    ln_final = ln_final_w.reshape(1, -1)
    return (_trunk(x_prompt, layers, ln_final), _trunk(x_sample, layers, ln_final))
```
